```python
import math
import jax
import jax.numpy as jnp
from jax import lax
import numpy as np

D_MODEL = 1024
BATCH = 8
SEQ = 4096
DEPTH = 4

EPS = 1e-6
POOL_WINDOWS = (2, 4, 8, 16)
POOL_GROUP = D_MODEL // 8
POOL_WIDTH = len(POOL_WINDOWS) * POOL_GROUP
DIL_PAIRS = ((128, 1), (512, 4), (2048, 16))
DIL_HEADS_PER_GROUP = 4
DIL_HEADS = len(DIL_PAIRS) * DIL_HEADS_PER_GROUP
DIL_HEAD_DIM = 64
DIL_QKV_WIDTH = DIL_HEADS * DIL_HEAD_DIM
DIL_OUT_WIDTH = DIL_HEADS_PER_GROUP * DIL_HEAD_DIM
REL_BUCKETS = 32
REL_MAX_DIST = 2048
SB_HEADS = 4
SB_HEAD_DIM = 128
SB_WIDTH = SB_HEADS * SB_HEAD_DIM
SB_BLOCK = 128
S5_WIDTH = D_MODEL // 2
S5_CH = 16
S5_GROUPS = S5_WIDTH // S5_CH
S5_STATE = 64
FFN_HIDDEN = ((8 * D_MODEL + 3 * 256 - 1) // (3 * 256)) * 256
N_BRANCHES = 4
IN_WIDTH = POOL_WIDTH + 3 * DIL_QKV_WIDTH + 3 * SB_WIDTH + S5_WIDTH
BRANCH_WIDTHS = (POOL_WIDTH, DIL_OUT_WIDTH, SB_WIDTH, S5_WIDTH)
BRANCH_WIDTH = sum(BRANCH_WIDTHS)

kernel_name = 'hybrid_gated_mixer_trunk'


def rmsnorm(x, g):
    xf = x.astype(jnp.float32)
    y = xf * lax.rsqrt(jnp.mean(xf * xf, axis=-1, keepdims=True) + EPS)
    return (y * g.astype(jnp.float32)).astype(x.dtype)


def pool_mixer(u, w_grp, scale):
    B_, S_, _ = u.shape
    uf = u.astype(jnp.float32)
    cs = jnp.concatenate([jnp.zeros_like(uf[:, :1]), jnp.cumsum(uf, axis=1)], axis=1)
    t = jnp.arange(S_)
    outs = []
    for gi, w in enumerate(POOL_WINDOWS):
        sl = slice(gi * POOL_GROUP, (gi + 1) * POOL_GROUP)
        lo = jnp.maximum(t + 1 - w, 0)
        cnt = (t + 1 - lo).astype(jnp.float32)
        mean = (cs[:, 1:, sl] - cs[:, lo, sl]) / cnt[None, :, None]
        outs.append(mean - uf[:, :, sl])
    p = jnp.stack(outs, axis=2)
    y = jnp.einsum('bsgc,gcd->bsgd', p, w_grp.astype(jnp.float32)).reshape(B_, S_, POOL_WIDTH)
    return (y * scale.astype(jnp.float32)).astype(u.dtype)


def t5_bucket(dist):
    exact = REL_BUCKETS // 2
    df = jnp.maximum(dist, 1).astype(jnp.float32)
    large = exact + (jnp.log(df / exact) / math.log(REL_MAX_DIST / exact)
                     * (REL_BUCKETS - exact)).astype(jnp.int32)
    large = jnp.minimum(large, REL_BUCKETS - 1)
    return jnp.where(dist < exact, dist, large)


def dilated_group(q, k, v, bias_tab, window, dil):
    B_, S_, H, E = q.shape
    band = window // dil
    L = S_ // dil
    nb = -(-L // band)
    Lp = nb * band

    def to_sub(a):
        a = a.reshape(B_, L, dil, H, E).transpose(0, 2, 3, 1, 4)
        return jnp.pad(a, ((0, 0), (0, 0), (0, 0), (0, Lp - L), (0, 0)))

    def band_keys(a):
        a = jnp.pad(a, ((0, 0), (0, 0), (0, 0), (band, 0), (0, 0))).reshape(B_, dil, H, nb + 1, band, E)
        return jnp.concatenate([a[:, :, :, :-1], a[:, :, :, 1:]], axis=4)

    qb = to_sub(q).reshape(B_, dil, H, nb, band, E)
    kb = band_keys(to_sub(k))
    vb = band_keys(to_sub(v))
    i = jnp.arange(band)[:, None]
    c = jnp.arange(2 * band)[None, :]
    dist_sub = band + i - c
    in_band = (dist_sub >= 0) & (dist_sub <= band)
    n = jnp.arange(nb)[:, None, None]
    mask = in_band[None] & ((n > 0) | (c[None] >= band))
    buckets = t5_bucket(jnp.clip(dist_sub, 0, band) * dil)
    bias = bias_tab.astype(jnp.float32)[buckets].transpose(2, 0, 1)
    s = jnp.einsum('bdhnqe,bdhnke->bdhnqk', qb, kb, preferred_element_type=jnp.float32) / math.sqrt(E)
    s = jnp.where(mask[None, None, None], s + bias[None, None, :, None], -1e30)
    m = jnp.max(s, axis=-1, keepdims=True)
    p = jnp.exp(s - m)
    den = jnp.sum(p, axis=-1, keepdims=True)
    o = jnp.einsum('bdhnqk,bdhnke->bdhnqe', p, vb.astype(jnp.float32)) / den
    lse = (m + jnp.log(den))[..., 0]
    o = o.reshape(B_, dil, H, Lp, E)[:, :, :, :L].transpose(0, 3, 1, 2, 4).reshape(B_, S_, H, E)
    lse = lse.reshape(B_, dil, H, Lp)[:, :, :, :L].transpose(0, 3, 1, 2).reshape(B_, S_, H)
    return o, lse


def dilated_attention(qkv, rel_bias):
    B_, S_ = qkv.shape[:2]
    outs, lses = [], []
    for g, (window, dil) in enumerate(DIL_PAIRS):
        hs = slice(g * DIL_HEADS_PER_GROUP, (g + 1) * DIL_HEADS_PER_GROUP)
        o, lse = dilated_group(qkv[:, :, 0, hs], qkv[:, :, 1, hs], qkv[:, :, 2, hs],
                               rel_bias[:, hs], window, dil)
        outs.append(o)
        lses.append(lse)
    o = jnp.stack(outs, axis=0)
    alpha = jax.nn.softmax(jnp.stack(lses, axis=0), axis=0)
    y = jnp.sum(alpha[..., None] * o, axis=0)
    return y.reshape(B_, S_, DIL_OUT_WIDTH).astype(qkv.dtype)


def stick_breaking_attention(qkv):
    B_, S_, _, H, E = qkv.shape
    nb = S_ // SB_BLOCK
    qh = qkv[:, :, 0].transpose(0, 2, 1, 3)
    kh = qkv[:, :, 1].transpose(0, 2, 1, 3)
    vh = qkv[:, :, 2].transpose(0, 2, 1, 3).astype(jnp.float32)
    qblk = qh.reshape(B_, H, nb, SB_BLOCK, E).transpose(2, 0, 1, 3, 4)
    key_pos = jnp.arange(S_)

    def one_block(args):
        qb, bi = args
        z = jnp.einsum('bhqe,bhke->bhqk', qb, kh, preferred_element_type=jnp.float32) / math.sqrt(E)
        q_pos = bi * SB_BLOCK + jnp.arange(SB_BLOCK)
        mask = key_pos[None, :] < q_pos[:, None]
        log_not = jnp.where(mask, jax.nn.log_sigmoid(-z), 0.0)
        excl = lax.cumsum(log_not, axis=3, reverse=True) - log_not
        w = jnp.where(mask, jnp.exp(jax.nn.log_sigmoid(z) + excl), 0.0)
        return jnp.einsum('bhqk,bhke->bhqe', w, vh)

    o = lax.map(one_block, (qblk, jnp.arange(nb)))
    return o.transpose(1, 0, 3, 2, 4).reshape(B_, S_, SB_WIDTH).astype(qkv.dtype)


def s5_mixer(u, a_re, a_im, log_dt, b_re, b_im, c_re, c_im, d_skip, w_glu):
    B_, S_, _ = u.shape
    f32 = jnp.float32
    uf = u.astype(f32).reshape(B_, S_, S5_GROUPS, S5_CH)
    lam = lax.complex(a_re.astype(f32), a_im.astype(f32))
    dt = jnp.exp(log_dt.astype(f32))[:, None]
    lam_bar = jnp.exp(lam * dt)
    b_t = lax.complex(b_re.astype(f32), b_im.astype(f32))
    b_bar = ((lam_bar - 1.0) / lam)[:, :, None] * b_t
    bu = jnp.einsum('gpc,bsgc->bsgp', b_bar, uf.astype(jnp.complex64))
    a = jnp.broadcast_to(lam_bar, (1, S_) + lam_bar.shape)

    def combine(left, right):
        a_l, b_l = left
        a_r, b_r = right
        return a_r * a_l, a_r * b_l + b_r

    _, h = lax.associative_scan(combine, (a, bu), axis=1)
    c_t = lax.complex(c_re.astype(f32), c_im.astype(f32))
    y = jnp.real(jnp.einsum('gcp,bsgp->bsgc', c_t, h)) + d_skip.astype(f32).reshape(S5_GROUPS, S5_CH) * uf
    y = jax.nn.gelu(y.reshape(B_, S_, S5_WIDTH))
    gl = y @ w_glu.astype(f32)
    y = gl[..., :S5_WIDTH] * jax.nn.sigmoid(gl[..., S5_WIDTH:])
    return y.astype(u.dtype)


def setup_inputs(seed: int = 0) -> dict:
    key = jax.random.key(seed)
    ks = jax.random.split(key, 22)
    nrm = jax.random.normal
    f32 = jnp.float32
    x = nrm(ks[0], (BATCH, SEQ, D_MODEL), f32)
    attn_norm_g = 1.0 + 0.02 * nrm(ks[1], (DEPTH, D_MODEL), f32)
    w_in = nrm(ks[2], (DEPTH, D_MODEL, IN_WIDTH), f32) * D_MODEL ** -0.5
    pool_w = nrm(ks[3], (DEPTH, len(POOL_WINDOWS), POOL_GROUP, POOL_GROUP), f32) * POOL_GROUP ** -0.5
    pool_scale = 1.0 + 0.02 * nrm(ks[4], (DEPTH, POOL_WIDTH), f32)
    rel_bias = 0.5 * nrm(ks[5], (REL_BUCKETS, DIL_HEADS), f32)
    s5_a_re = -0.5 + 0.01 * nrm(ks[6], (DEPTH, S5_GROUPS, S5_STATE), f32)
    s5_a_im = jnp.pi * jnp.arange(S5_STATE, dtype=f32)[None, None, :] + 0.01 * nrm(ks[7], (DEPTH, S5_GROUPS, S5_STATE), f32)
    s5_log_dt = jax.random.uniform(ks[8], (DEPTH, S5_GROUPS), f32, minval=math.log(1e-3), maxval=math.log(1e-1))
    s5_b_re = nrm(ks[9], (DEPTH, S5_GROUPS, S5_STATE, S5_CH), f32) * (2 * S5_CH) ** -0.5
    s5_b_im = nrm(ks[10], (DEPTH, S5_GROUPS, S5_STATE, S5_CH), f32) * (2 * S5_CH) ** -0.5
    s5_c_re = nrm(ks[11], (DEPTH, S5_GROUPS, S5_CH, S5_STATE), f32) * S5_STATE ** -0.5
    s5_c_im = nrm(ks[12], (DEPTH, S5_GROUPS, S5_CH, S5_STATE), f32) * S5_STATE ** -0.5
    s5_d = nrm(ks[13], (DEPTH, S5_WIDTH), f32)
    s5_w_glu = nrm(ks[14], (DEPTH, S5_WIDTH, 2 * S5_WIDTH), f32) * S5_WIDTH ** -0.5
    row_scale = jnp.concatenate([jnp.full((w,), w ** -0.5, f32) for w in BRANCH_WIDTHS])
    w_branch = nrm(ks[15], (DEPTH, BRANCH_WIDTH, D_MODEL), f32) * row_scale[None, :, None]
    w_gate = nrm(ks[16], (DEPTH, N_BRANCHES, D_MODEL, D_MODEL), f32) * D_MODEL ** -0.5
    w_out = nrm(ks[17], (DEPTH, D_MODEL, D_MODEL), f32) * D_MODEL ** -0.5
    ffn_norm_g = 1.0 + 0.02 * nrm(ks[18], (DEPTH, D_MODEL), f32)
    w_up = nrm(ks[19], (DEPTH, D_MODEL, 2 * FFN_HIDDEN), f32) * D_MODEL ** -0.5
    w_down = nrm(ks[20], (DEPTH, FFN_HIDDEN, D_MODEL), f32) * FFN_HIDDEN ** -0.5
    final_norm_g = 1.0 + 0.02 * nrm(ks[21], (D_MODEL,), f32)
    return {'x': x, 'attn_norm_g': attn_norm_g, 'w_in': w_in, 'pool_w': pool_w,
            'pool_scale': pool_scale, 'rel_bias': rel_bias, 's5_a_re': s5_a_re,
            's5_a_im': s5_a_im, 's5_log_dt': s5_log_dt, 's5_b_re': s5_b_re, 's5_b_im': s5_b_im,
            's5_c_re': s5_c_re, 's5_c_im': s5_c_im, 's5_d': s5_d, 's5_w_glu': s5_w_glu,
            'w_branch': w_branch, 'w_gate': w_gate, 'w_out': w_out, 'ffn_norm_g': ffn_norm_g,
            'w_up': w_up, 'w_down': w_down, 'final_norm_g': final_norm_g}


def reference(x, attn_norm_g, w_in, pool_w, pool_scale, rel_bias, s5_a_re, s5_a_im, s5_log_dt,
              s5_b_re, s5_b_im, s5_c_re, s5_c_im, s5_d, s5_w_glu, w_branch, w_gate, w_out,
              ffn_norm_g, w_up, w_down, final_norm_g):
    B_, S_, _ = x.shape
    o1 = POOL_WIDTH
    o2 = o1 + 3 * DIL_QKV_WIDTH
    o3 = o2 + 3 * SB_WIDTH
    for l in range(DEPTH):
        xn = rmsnorm(x, attn_norm_g[l])
        proj = xn @ w_in[l]
        u_pool = proj[..., :o1]
        qkv_dil = proj[..., o1:o2].reshape(B_, S_, 3, DIL_HEADS, DIL_HEAD_DIM)
        qkv_sb = proj[..., o2:o3].reshape(B_, S_, 3, SB_HEADS, SB_HEAD_DIM)
        u_s5 = proj[..., o3:]
        y_pool = pool_mixer(u_pool, pool_w[l], pool_scale[l])
        y_dil = dilated_attention(qkv_dil, rel_bias)
        y_sb = stick_breaking_attention(qkv_sb)
        y_s5 = s5_mixer(u_s5, s5_a_re[l], s5_a_im[l], s5_log_dt[l], s5_b_re[l], s5_b_im[l],
                        s5_c_re[l], s5_c_im[l], s5_d[l], s5_w_glu[l])
        merged = jnp.zeros_like(x)
        row = 0
        for bi, yb in enumerate((y_pool, y_dil, y_sb, y_s5)):
            width = BRANCH_WIDTHS[bi]
            gate = jax.nn.sigmoid(xn @ w_gate[l, bi])
            merged = merged + gate * (yb @ w_branch[l, row:row + width])
            row += width
        x = x + merged @ w_out[l]
        hn = rmsnorm(x, ffn_norm_g[l])
        gu = hn @ w_up[l]
        h = jax.nn.silu(gu[..., :FFN_HIDDEN]) * gu[..., FFN_HIDDEN:]
        x = x + h @ w_down[l]
    return rmsnorm(x, final_norm_g)
```

```python
import functools
import math

import jax
import jax.numpy as jnp
from jax import lax
from jax.experimental import pallas as pl
from jax.experimental.pallas import tpu as pltpu

F32 = jnp.float32
BF16 = jnp.bfloat16

EPS = 1e-6
POOL_WINDOWS = (2, 4, 8, 16)
POOL_GROUP = 128
POOL_WIDTH = 512
POOL_HALO = 16
DIL_PAIRS = ((128, 1), (512, 4), (2048, 16))
DIL_BAND = 128
DIL_HEADS_PER_GROUP = 4
DIL_HEADS = 12
DIL_HEAD_DIM = 64
DIL_GROUP_WIDTH = DIL_HEADS_PER_GROUP * DIL_HEAD_DIM
REL_BUCKETS = 32
REL_MAX_DIST = 2048
SB_HEADS = 4
SB_HEAD_DIM = 128
SB_WIDTH = 512
SB_TILE = 256
S5_WIDTH = 512
S5_CH = 16
S5_GROUPS = 32
S5_STATE = 64
S5_CHUNKS = 4
S5_CHUNK_CH = S5_WIDTH // S5_CHUNKS
S5_CHUNK_STATE = (S5_GROUPS // S5_CHUNKS) * S5_STATE
BRANCH_ROWS = (0, 512, 768, 1280, 1792)
NEG_INF = -1e30
LANES = 128
LOG2E = math.log2(math.e)

VMEM_LIMIT = 56 * 1024 * 1024


def _cparams(n_axes):
    return pltpu.CompilerParams(dimension_semantics=("arbitrary",) * n_axes,
                                vmem_limit_bytes=VMEM_LIMIT)


def _dot(a, b):
    return jnp.dot(a, b, preferred_element_type=F32)


def _dot_nt(a, b):
    return lax.dot_general(a, b, (((1,), (1,)), ((), ())), preferred_element_type=F32)


def _rms(x, g):
    ms = jnp.mean(x * x, axis=-1, keepdims=True)
    return x * lax.rsqrt(ms + EPS) * g


def _in_proj_kernel(x_ref, g_ref, wpool_ref, wd0_ref, wd1_ref, wd2_ref, wq_ref, wk_ref, wvt_ref,
                    ws5_ref, pool_ref, d0_ref, d1_ref, d2_ref, q_ref, k_ref, vt_ref, s5_ref,
                    scr_ref, *, tm, sb_scale):
    xn = _rms(x_ref[...], g_ref[...]).astype(BF16)
    pool_ref[...] = _dot(xn, wpool_ref[...])
    s5_ref[...] = _dot(xn, ws5_ref[...])

    def dil_proj(w_ref):
        q = _dot(xn, w_ref[:, :DIL_GROUP_WIDTH]) * (1.0 / math.sqrt(DIL_HEAD_DIM))
        kv = _dot(xn, w_ref[:, DIL_GROUP_WIDTH:])
        return q, kv

    q0, kv0 = dil_proj(wd0_ref)
    d0_ref[:, :DIL_GROUP_WIDTH] = q0.astype(BF16)
    d0_ref[:, DIL_GROUP_WIDTH:] = kv0.astype(BF16)
    for w_ref, out_ref, dil in ((wd1_ref, d1_ref, DIL_PAIRS[1][1]), (wd2_ref, d2_ref, DIL_PAIRS[2][1])):
        qg, kvg = dil_proj(w_ref)
        qkv = jnp.concatenate([qg, kvg], axis=1)
        for sl in range(qkv.shape[1] // LANES):
            scr_ref[sl] = qkv[:, sl * LANES:(sl + 1) * LANES]
        for r in range(dil):
            for sl in range(qkv.shape[1] // LANES):
                out_ref[r, :, sl * LANES:(sl + 1) * LANES] = (
                    scr_ref[sl, pl.ds(r, tm // dil, stride=dil), :].astype(BF16))

    q_ref[...] = (_dot(xn, wq_ref[...]) * sb_scale).astype(BF16)
    k_ref[...] = _dot(xn, wk_ref[...]).astype(BF16)
    vt = _dot_nt(wvt_ref[...], xn).astype(BF16)
    for jb in range(tm // SB_TILE):
        vt_ref[jb] = vt[:, jb * SB_TILE:(jb + 1) * SB_TILE]


def _in_proj(x, g, wpool, wd0, wd1, wd2, wq, wk, wvt, ws5, *, tm):
    b, s, d = x.shape
    dil1, dil2 = DIL_PAIRS[1][1], DIL_PAIRS[2][1]
    gw = 3 * DIL_GROUP_WIDTH
    const = lambda shape: pl.BlockSpec(shape, lambda bi, ti: (0,) * len(shape))
    out_shape = (
        jax.ShapeDtypeStruct((b, s, POOL_WIDTH), F32),
        jax.ShapeDtypeStruct((b, s, gw), BF16),
        jax.ShapeDtypeStruct((b, dil1, s // dil1, gw), BF16),
        jax.ShapeDtypeStruct((b, dil2, s // dil2, gw), BF16),
        jax.ShapeDtypeStruct((b, s, SB_WIDTH), BF16),
        jax.ShapeDtypeStruct((b, s, SB_WIDTH), BF16),
        jax.ShapeDtypeStruct((b, s // SB_TILE, SB_WIDTH, SB_TILE), BF16),
        jax.ShapeDtypeStruct((b, s, S5_WIDTH), F32),
    )
    out_specs = (
        pl.BlockSpec((None, tm, POOL_WIDTH), lambda bi, ti: (bi, ti, 0)),
        pl.BlockSpec((None, tm, gw), lambda bi, ti: (bi, ti, 0)),
        pl.BlockSpec((None, dil1, tm // dil1, gw), lambda bi, ti: (bi, 0, ti, 0)),
        pl.BlockSpec((None, dil2, tm // dil2, gw), lambda bi, ti: (bi, 0, ti, 0)),
        pl.BlockSpec((None, tm, SB_WIDTH), lambda bi, ti: (bi, ti, 0)),
        pl.BlockSpec((None, tm, SB_WIDTH), lambda bi, ti: (bi, ti, 0)),
        pl.BlockSpec((None, tm // SB_TILE, SB_WIDTH, SB_TILE), lambda bi, ti: (bi, ti, 0, 0)),
        pl.BlockSpec((None, tm, S5_WIDTH), lambda bi, ti: (bi, ti, 0)),
    )
    in_specs = [
        pl.BlockSpec((None, tm, d), lambda bi, ti: (bi, ti, 0)),
        const(g.shape), const(wpool.shape), const(wd0.shape), const(wd1.shape), const(wd2.shape),
        const(wq.shape), const(wk.shape), const(wvt.shape), const(ws5.shape),
    ]
    sb_scale = LOG2E / math.sqrt(SB_HEAD_DIM)
    return pl.pallas_call(
        functools.partial(_in_proj_kernel, tm=tm, sb_scale=sb_scale),
        grid=(b, s // tm), in_specs=in_specs, out_specs=out_specs, out_shape=out_shape,
        scratch_shapes=[pltpu.VMEM((gw // LANES, tm, LANES), F32)],
        compiler_params=_cparams(2), name="in_proj",
    )(x, g, wpool, wd0, wd1, wd2, wq, wk, wvt, ws5)


def _dil_attn_kernel(cur_ref, halo_ref, bias_ref, o_ref, lse_ref, kv_ref, *, rows):
    nblk = rows // DIL_BAND
    gw = DIL_GROUP_WIDTH
    step = pl.program_id(2)
    kv_ref[:DIL_BAND, :] = halo_ref[:, gw:]
    kv_ref[DIL_BAND:, :] = cur_ref[:, gw:]
    lane_head = lax.broadcasted_iota(jnp.int32, (1, gw), 1) // DIL_HEAD_DIM

    def block(n, carry):
        r0 = pl.multiple_of(n * DIL_BAND, DIL_BAND)
        q = cur_ref[pl.ds(r0, DIL_BAND), :gw]
        kvb = kv_ref[pl.ds(r0, 2 * DIL_BAND), :]
        k = kvb[:, :gw]
        v = kvb[:, gw:]
        first = (step * nblk + n == 0).astype(jnp.int32)
        acc = jnp.zeros((DIL_BAND, gw), F32)
        den_b = jnp.ones((DIL_BAND, gw), F32)
        m_b = jnp.zeros((DIL_BAND, gw), F32)
        for h in range(DIL_HEADS_PER_GROUP):
            hm = lane_head == h
            hmul = hm.astype(BF16)
            s = _dot_nt(q * hmul, k) + bias_ref[first, h]
            m = jnp.max(s, axis=-1, keepdims=True)
            p = jnp.exp(s - m)
            den = jnp.sum(p, axis=-1, keepdims=True)
            acc = acc + _dot(p.astype(BF16), v * hmul)
            den_b = jnp.where(hm, den, den_b)
            m_b = jnp.where(hm, m, m_b)
        o_ref[pl.ds(r0, DIL_BAND), :] = acc / den_b
        lse_ref[pl.ds(r0, DIL_BAND), :] = m_b + jnp.log(den_b)
        return carry

    lax.fori_loop(0, nblk, block, 0)


def _dil_attn(qkv, bias):
    b, dil, length, gw3 = qkv.shape
    rows = min(length, 1024)
    rb = rows // DIL_BAND
    out_shape = (jax.ShapeDtypeStruct((b, dil, length, DIL_GROUP_WIDTH), F32),) * 2
    blk = lambda w: pl.BlockSpec((None, None, rows, w), lambda bi, ri, i: (bi, ri, i, 0))
    halo = pl.BlockSpec((None, None, DIL_BAND, gw3),
                        lambda bi, ri, i: (bi, ri, jnp.maximum(i * rb - 1, 0), 0))
    return pl.pallas_call(
        functools.partial(_dil_attn_kernel, rows=rows),
        grid=(b, dil, length // rows),
        in_specs=[blk(gw3), halo, pl.BlockSpec(bias.shape, lambda bi, ri, i: (0, 0, 0, 0))],
        out_specs=(blk(DIL_GROUP_WIDTH), blk(DIL_GROUP_WIDTH)), out_shape=out_shape,
        scratch_shapes=[pltpu.VMEM((rows + DIL_BAND, 2 * DIL_GROUP_WIDTH), BF16)],
        compiler_params=_cparams(3), name=f"dil_attn_d{dil}",
    )(qkv, qkv, bias)


def _sb_attn_kernel(q_ref, k_ref, vt_ref, tri_ref, o_ref):
    t = SB_TILE
    i = pl.program_id(2)
    q = q_ref[...]
    tri = tri_ref[...]
    row = lax.broadcasted_iota(jnp.int32, (t, t), 0)
    col = lax.broadcasted_iota(jnp.int32, (t, t), 1)
    causal = row < col

    def tile(j, run, acc, mask):
        ks = pl.multiple_of(j * t, t)
        z = _dot_nt(k_ref[pl.ds(ks, t), :], q)
        sp = jnp.maximum(z, 0.0) + jnp.log(1.0 + jnp.exp2(-jnp.abs(z))) * LOG2E
        if mask is not None:
            sp = jnp.where(mask, sp, 0.0)
        hi = sp.astype(BF16)
        lo = (sp - hi.astype(F32)).astype(BF16)
        incl = _dot(tri, hi) + _dot(tri, lo)
        w = jnp.exp2(z - incl - run)
        if mask is not None:
            w = jnp.where(mask, w, 0.0)
        acc = acc + _dot(vt_ref[j], w.astype(BF16))
        return run + incl[0:1, :], acc

    run0 = jnp.zeros((1, t), F32)
    acc0 = jnp.zeros((SB_HEAD_DIM, t), F32)
    run, acc = tile(i, run0, acc0, causal)

    def body(jj, carry):
        return tile(i - jj, carry[0], carry[1], None)

    run, acc = lax.fori_loop(1, i + 1, body, (run, acc))
    o_ref[...] = acc.T.astype(o_ref.dtype)


def _sb_attn(q, k, vt, tri):
    b, s, _ = q.shape
    t = SB_TILE
    return pl.pallas_call(
        _sb_attn_kernel,
        grid=(b, SB_HEADS, s // t),
        in_specs=[
            pl.BlockSpec((None, t, SB_HEAD_DIM), lambda bi, h, i: (bi, i, h)),
            pl.BlockSpec((None, s, SB_HEAD_DIM), lambda bi, h, i: (bi, 0, h)),
            pl.BlockSpec((None, s // t, SB_HEAD_DIM, t), lambda bi, h, i: (bi, 0, h, 0)),
            pl.BlockSpec((t, t), lambda bi, h, i: (0, 0)),
        ],
        out_specs=pl.BlockSpec((None, t, SB_HEAD_DIM), lambda bi, h, i: (bi, i, h)),
        out_shape=jax.ShapeDtypeStruct((b, s, SB_WIDTH), BF16),
        compiler_params=_cparams(3), name="sb_attn",
    )(q, k, vt, tri)


def _gelu_tanh(x):
    return 0.5 * x * (1.0 + jnp.tanh(math.sqrt(2.0 / math.pi) * (x + 0.044715 * (x * x * x))))


def _s5_kernel(u_ref, lr_ref, li_ref, wb_ref, wc_ref, d_ref, wglu_ref, out_ref, bu_ref, h_ref, *, steps):
    cs = S5_CHUNK_STATE

    @pl.when(pl.program_id(0) == 0)
    def _():
        h_ref[...] = jnp.zeros_like(h_ref)

    u = u_ref[...]
    ub = u.astype(BF16)
    for c in range(S5_CHUNKS):
        bu_ref[:, 2 * cs * c:2 * cs * (c + 1)] = _dot(ub[:, S5_CHUNK_CH * c:S5_CHUNK_CH * (c + 1)], wb_ref[c])

    nb = h_ref.shape[0]
    for c0 in range(0, S5_CHUNKS, 2):
        lam = [(jnp.broadcast_to(lr_ref[c], (nb, cs)), jnp.broadcast_to(li_ref[c], (nb, cs)))
               for c in (c0, c0 + 1)]
        offs = [2 * cs * c for c in (c0, c0 + 1)]

        def step(t, carry):
            r = pl.multiple_of(t * nb, nb)
            new = []
            for (lr, li), off, (hr, hi) in zip(lam, offs, carry):
                br = bu_ref[pl.ds(r, nb), off:off + cs]
                bi = bu_ref[pl.ds(r, nb), off + cs:off + 2 * cs]
                nhr = lr * hr - li * hi + br
                nhi = lr * hi + li * hr + bi
                bu_ref[pl.ds(r, nb), off:off + cs] = nhr
                bu_ref[pl.ds(r, nb), off + cs:off + 2 * cs] = nhi
                new.append((nhr, nhi))
            return tuple(new)

        init = tuple((h_ref[:, off:off + cs], h_ref[:, off + cs:off + 2 * cs]) for off in offs)
        fin = lax.fori_loop(0, steps, step, init)
        for off, (hr, hi) in zip(offs, fin):
            h_ref[:, off:off + cs] = hr
            h_ref[:, off + cs:off + 2 * cs] = hi

    y = jnp.concatenate(
        [_dot(bu_ref[:, 2 * cs * c:2 * cs * (c + 1)].astype(BF16), wc_ref[c]) for c in range(S5_CHUNKS)],
        axis=1)
    y = _gelu_tanh(y + d_ref[...] * u)
    gl = _dot(y.astype(BF16), wglu_ref[...])
    out_ref[...] = gl[:, :S5_WIDTH] * jax.nn.sigmoid(gl[:, S5_WIDTH:])


def _s5(u2d, lr, li, wb, wc, d, wglu, *, batch, steps):
    n, _ = u2d.shape
    rows = steps * batch
    const = lambda a: pl.BlockSpec(a.shape, lambda i: (0,) * a.ndim)
    return pl.pallas_call(
        functools.partial(_s5_kernel, steps=steps),
        grid=(n // rows,),
        in_specs=[pl.BlockSpec((rows, S5_WIDTH), lambda i: (i, 0)),
                  const(lr), const(li), const(wb), const(wc), const(d), const(wglu)],
        out_specs=pl.BlockSpec((rows, S5_WIDTH), lambda i: (i, 0)),
        out_shape=jax.ShapeDtypeStruct((n, S5_WIDTH), F32),
        scratch_shapes=[pltpu.VMEM((rows, 2 * S5_CHUNKS * S5_CHUNK_STATE), F32),
                        pltpu.VMEM((batch, 2 * S5_CHUNKS * S5_CHUNK_STATE), F32)],
        compiler_params=_cparams(1), name="s5_mixer",
    )(u2d, lr, li, wb, wc, d, wglu)


def _merge_kernel(x_ref, g_ref, up_ref, o0_ref, l0_ref, o1_ref, l1_ref, o2_ref, l2_ref, ysb_ref, ys5_ref,
                  pw_ref, ps_ref, wg_ref, wb_ref, wo_ref, out_ref, ext_ref, no1_ref, nl1_ref, no2_ref, nl2_ref,
                  *, tm):
    ti = pl.program_id(1)
    x = x_ref[...]
    xn = _rms(x, g_ref[...]).astype(BF16)

    @pl.when(ti == 0)
    def _():
        ext_ref[:POOL_HALO, :] = jnp.zeros((POOL_HALO, POOL_WIDTH), F32)

    ext_ref[POOL_HALO:, :] = up_ref[...]
    pos = ti * tm + lax.broadcasted_iota(jnp.int32, (tm, 1), 0)
    ys = []
    for gi, w in enumerate(POOL_WINDOWS):
        c0, c1 = gi * POOL_GROUP, (gi + 1) * POOL_GROUP
        acc = ext_ref[pl.ds(POOL_HALO, tm), c0:c1]
        for j in range(1, w):
            acc = acc + ext_ref[pl.ds(POOL_HALO - j, tm), c0:c1]
        cnt = jnp.minimum(pos + 1, w).astype(F32)
        p = acc / cnt - ext_ref[pl.ds(POOL_HALO, tm), c0:c1]
        ys.append(_dot(p.astype(BF16), pw_ref[gi]))
    y_pool = jnp.concatenate(ys, axis=1) * ps_ref[...]
    ext_ref[:POOL_HALO, :] = ext_ref[pl.ds(tm, POOL_HALO), :]

    nslab = DIL_GROUP_WIDTH // LANES

    def to_token_order(src_ref, dst_ref):
        dil = src_ref.shape[0]
        for r in range(dil):
            for sl in range(nslab):
                dst_ref[sl, pl.ds(r, tm // dil, stride=dil), :] = src_ref[r, :, sl * LANES:(sl + 1) * LANES]
        return jnp.concatenate([dst_ref[sl] for sl in range(nslab)], axis=1)

    o1, l1 = to_token_order(o1_ref, no1_ref), to_token_order(l1_ref, nl1_ref)
    o2, l2 = to_token_order(o2_ref, no2_ref), to_token_order(l2_ref, nl2_ref)
    l0 = l0_ref[...]
    m = jnp.maximum(jnp.maximum(l0, l1), l2)
    e0, e1, e2 = jnp.exp(l0 - m), jnp.exp(l1 - m), jnp.exp(l2 - m)
    y_dil = (e0 * o0_ref[...] + e1 * o1 + e2 * o2) / (e0 + e1 + e2)

    branches = (y_pool.astype(BF16), y_dil.astype(BF16), ysb_ref[...], ys5_ref[...].astype(BF16))
    merged = jnp.zeros((tm, x.shape[1]), F32)
    for bi, yb in enumerate(branches):
        gate = jax.nn.sigmoid(_dot(xn, wg_ref[bi]))
        merged = merged + gate * _dot(yb, wb_ref[BRANCH_ROWS[bi]:BRANCH_ROWS[bi + 1], :])
    out_ref[...] = x + _dot(merged.astype(BF16), wo_ref[...])


def _merge(x, g, up, o0, l0, o1, l1, o2, l2, ysb, ys5, pw, ps, wg, wb, wo, *, tm):
    b, s, d = x.shape
    dil1, dil2 = o1.shape[1], o2.shape[1]
    gw = DIL_GROUP_WIDTH
    tok = lambda w: pl.BlockSpec((None, tm, w), lambda bi, ti: (bi, ti, 0))
    res = lambda dil: pl.BlockSpec((None, dil, tm // dil, gw), lambda bi, ti: (bi, 0, ti, 0))
    const = lambda a: pl.BlockSpec(a.shape, lambda bi, ti: (0,) * a.ndim)
    in_specs = [tok(d), const(g), tok(POOL_WIDTH), tok(gw), tok(gw), res(dil1), res(dil1), res(dil2), res(dil2),
                tok(SB_WIDTH), tok(S5_WIDTH),
                const(pw), const(ps), const(wg), const(wb), const(wo)]
    return pl.pallas_call(
        functools.partial(_merge_kernel, tm=tm),
        grid=(b, s // tm), in_specs=in_specs, out_specs=tok(d),
        out_shape=jax.ShapeDtypeStruct((b, s, d), F32),
        scratch_shapes=[pltpu.VMEM((tm + POOL_HALO, POOL_WIDTH), F32)]
        + [pltpu.VMEM((gw // LANES, tm, LANES), F32)] * 4,
        compiler_params=_cparams(2), name="gated_merge",
    )(x, g, up, o0, l0, o1, l1, o2, l2, ysb, ys5, pw, ps, wg, wb, wo)


def _ffn_kernel(x_ref, g_ref, wup_ref, wdn_ref, fg_ref, out_ref, *, hidden, final):
    x = x_ref[...]
    hn = _rms(x, g_ref[...]).astype(BF16)
    gu = _dot(hn, wup_ref[...])
    h = (jax.nn.silu(gu[:, :hidden]) * gu[:, hidden:]).astype(BF16)
    y = x + _dot(h, wdn_ref[...])
    if final:
        y = _rms(y, fg_ref[...])
    out_ref[...] = y


def _ffn(x2d, g, wup, wdn, fg, *, tm, final):
    n, d = x2d.shape
    hidden = wdn.shape[0]
    const = lambda a: pl.BlockSpec(a.shape, lambda i: (0,) * a.ndim)
    return pl.pallas_call(
        functools.partial(_ffn_kernel, hidden=hidden, final=final),
        grid=(n // tm,),
        in_specs=[pl.BlockSpec((tm, d), lambda i: (i, 0)), const(g), const(wup), const(wdn), const(fg)],
        out_specs=pl.BlockSpec((tm, d), lambda i: (i, 0)),
        out_shape=jax.ShapeDtypeStruct((n, d), F32),
        compiler_params=_cparams(1), name="swiglu_final" if final else "swiglu",
    )(x2d, g, wup, wdn, fg)


def _t5_bucket(dist):
    exact = REL_BUCKETS // 2
    df = jnp.maximum(dist, 1).astype(F32)
    large = exact + (jnp.log(df / exact) / math.log(REL_MAX_DIST / exact)
                     * (REL_BUCKETS - exact)).astype(jnp.int32)
    large = jnp.minimum(large, REL_BUCKETS - 1)
    return jnp.where(dist < exact, dist, large)


def _dil_bias(rel_bias_g, dil):
    band = DIL_BAND
    i = jnp.arange(band)[:, None]
    c = jnp.arange(2 * band)[None, :]
    dist_sub = band + i - c
    in_band = (dist_sub >= 0) & (dist_sub <= band)
    buckets = _t5_bucket(jnp.clip(dist_sub, 0, band) * dil)
    bias = rel_bias_g.astype(F32)[buckets].transpose(2, 0, 1)
    bias = jnp.where(in_band[None], bias, NEG_INF)
    first = jnp.where((c >= band)[None], bias, NEG_INF)
    return jnp.stack([bias, first])


def _dil_weights(w_dil, group):
    hw = DIL_HEADS * DIL_HEAD_DIM
    cols = [w_dil[:, i * hw + group * DIL_GROUP_WIDTH:i * hw + (group + 1) * DIL_GROUP_WIDTH] for i in range(3)]
    return jnp.concatenate(cols, axis=1).astype(BF16)


def _s5_params(a_re, a_im, log_dt, b_re, b_im, c_re, c_im):
    lam = lax.complex(a_re.astype(F32), a_im.astype(F32))
    dt = jnp.exp(log_dt.astype(F32))[:, None]
    lam_bar = jnp.exp(lam * dt)
    b_bar = ((lam_bar - 1.0) / lam)[:, :, None] * lax.complex(b_re.astype(F32), b_im.astype(F32))
    gpc = S5_GROUPS // S5_CHUNKS
    eye = jnp.eye(gpc, dtype=F32)

    def pack_b(bm):
        bm = bm.reshape(S5_CHUNKS, gpc, S5_STATE, S5_CH)
        return jnp.einsum('cgpk,gh->cgkhp', bm, eye).reshape(S5_CHUNKS, S5_CHUNK_CH, S5_CHUNK_STATE)

    def pack_c(cm):
        cm = cm.reshape(S5_CHUNKS, gpc, S5_CH, S5_STATE)
        return jnp.einsum('cgkp,gh->cgphk', cm, eye).reshape(S5_CHUNKS, S5_CHUNK_STATE, S5_CHUNK_CH)

    wb = jnp.concatenate([pack_b(jnp.real(b_bar)), pack_b(jnp.imag(b_bar))], axis=2).astype(BF16)
    wc = jnp.concatenate([pack_c(c_re.astype(F32)), pack_c(-c_im.astype(F32))], axis=1).astype(BF16)
    lr = jnp.real(lam_bar).reshape(S5_CHUNKS, 1, S5_CHUNK_STATE)
    li = jnp.imag(lam_bar).reshape(S5_CHUNKS, 1, S5_CHUNK_STATE)
    return lr, li, wb, wc


def kernel(x, attn_norm_g, w_in, pool_w, pool_scale, rel_bias, s5_a_re, s5_a_im, s5_log_dt, s5_b_re, s5_b_im,
           s5_c_re, s5_c_im, s5_d, s5_w_glu, w_branch, w_gate, w_out, ffn_norm_g, w_up, w_down, final_norm_g):
    b, s, d = x.shape
    depth = w_in.shape[0]
    tm = 512
    o1 = POOL_WIDTH
    o2 = o1 + 3 * DIL_HEADS * DIL_HEAD_DIM
    o3 = o2 + 3 * SB_WIDTH

    t = SB_TILE
    tri = (jnp.arange(t)[None, :] >= jnp.arange(t)[:, None]).astype(BF16)
    biases = [_dil_bias(rel_bias[:, g * DIL_HEADS_PER_GROUP:(g + 1) * DIL_HEADS_PER_GROUP], dil)
              for g, (_, dil) in enumerate(DIL_PAIRS)]
    fg = final_norm_g.reshape(1, d).astype(F32)

    for l in range(depth):
        wl = w_in[l]
        g_attn = attn_norm_g[l].reshape(1, d).astype(F32)
        w_dil = wl[:, o1:o2]
        w_sb = wl[:, o2:o3]
        up, d0, d1, d2, q, k, vt, u_s5 = _in_proj(
            x, g_attn, wl[:, :o1].astype(BF16),
            _dil_weights(w_dil, 0), _dil_weights(w_dil, 1), _dil_weights(w_dil, 2),
            w_sb[:, :SB_WIDTH].astype(BF16), w_sb[:, SB_WIDTH:2 * SB_WIDTH].astype(BF16),
            w_sb[:, 2 * SB_WIDTH:].T.astype(BF16), wl[:, o3:].astype(BF16), tm=tm)

        oa0, la0 = _dil_attn(d0.reshape(b, 1, s, d0.shape[-1]), biases[0])
        oa1, la1 = _dil_attn(d1, biases[1])
        oa2, la2 = _dil_attn(d2, biases[2])
        y_sb = _sb_attn(q, k, vt, tri)

        lr, li, wb, wc = _s5_params(s5_a_re[l], s5_a_im[l], s5_log_dt[l], s5_b_re[l], s5_b_im[l],
                                    s5_c_re[l], s5_c_im[l])
        y_s5 = _s5(jnp.swapaxes(u_s5, 0, 1).reshape(s * b, S5_WIDTH), lr, li, wb, wc,
                   s5_d[l].reshape(1, S5_WIDTH).astype(F32), s5_w_glu[l].astype(BF16), batch=b, steps=64)
        y_s5 = jnp.swapaxes(y_s5.reshape(s, b, S5_WIDTH), 0, 1)

        x = _merge(x, g_attn, up, oa0.reshape(b, s, -1), la0.reshape(b, s, -1), oa1, la1, oa2, la2,
                   y_sb, y_s5,
                   pool_w[l].astype(BF16), pool_scale[l].reshape(1, POOL_WIDTH).astype(F32),
                   w_gate[l].astype(BF16), w_branch[l].astype(BF16), w_out[l].astype(BF16), tm=256)

        x = _ffn(x.reshape(b * s, d), ffn_norm_g[l].reshape(1, d).astype(F32), w_up[l].astype(BF16),
                 w_down[l].astype(BF16), fg, tm=256, final=(l == depth - 1)).reshape(b, s, d)
    return x
```

```python
import functools
import math

import jax
import jax.numpy as jnp
from jax import lax
from jax.experimental import pallas as pl
from jax.experimental.pallas import tpu as pltpu

F32 = jnp.float32
BF16 = jnp.bfloat16

EPS = 1e-6
POOL_WINDOWS = (2, 4, 8, 16)
POOL_GROUP = 128
POOL_WIDTH = 512
POOL_HALO = 16
DIL_PAIRS = ((128, 1), (512, 4), (2048, 16))
DIL_BAND = 128
DIL_HEADS_PER_GROUP = 4
DIL_HEADS = 12
DIL_HEAD_DIM = 64
DIL_GROUP_WIDTH = DIL_HEADS_PER_GROUP * DIL_HEAD_DIM
REL_BUCKETS = 32
REL_MAX_DIST = 2048
SB_HEADS = 4
SB_HEAD_DIM = 128
SB_WIDTH = 512
SB_TILE = 256
S5_WIDTH = 512
S5_CH = 16
S5_GROUPS = 32
S5_STATE = 64
S5_CHUNKS = 4
S5_CHUNK_CH = S5_WIDTH // S5_CHUNKS
S5_CHUNK_STATE = (S5_GROUPS // S5_CHUNKS) * S5_STATE
BRANCH_ROWS = (0, 512, 768, 1280, 1792)
NEG_INF = -1e30
LANES = 128
LOG2E = math.log2(math.e)

VMEM_LIMIT = 56 * 1024 * 1024


def _cparams(n_axes):
    return pltpu.CompilerParams(dimension_semantics=("arbitrary",) * n_axes,
                                vmem_limit_bytes=VMEM_LIMIT)


def _dot(a, b):
    return jnp.dot(a, b, preferred_element_type=F32)


def _dot_nt(a, b):
    return lax.dot_general(a, b, (((1,), (1,)), ((), ())), preferred_element_type=F32)


def _rms(x, g):
    ms = jnp.mean(x * x, axis=-1, keepdims=True)
    return x * lax.rsqrt(ms + EPS) * g


def _in_proj_kernel(x_ref, g_ref, wpool_ref, wd0_ref, wd1_ref, wd2_ref, wq_ref, wk_ref, wvt_ref,
                    ws5_ref, pool_ref, d0_ref, d1_ref, d2_ref, q_ref, k_ref, vt_ref, s5_ref,
                    scr_ref, *, tm, sb_scale):
    xn = _rms(x_ref[...], g_ref[...]).astype(BF16)
    pool_ref[...] = _dot(xn, wpool_ref[...])
    s5_ref[...] = _dot(xn, ws5_ref[...])

    def dil_proj(w_ref):
        q = _dot(xn, w_ref[:, :DIL_GROUP_WIDTH]) * (1.0 / math.sqrt(DIL_HEAD_DIM))
        kv = _dot(xn, w_ref[:, DIL_GROUP_WIDTH:])
        return q, kv

    q0, kv0 = dil_proj(wd0_ref)
    d0_ref[:, :DIL_GROUP_WIDTH] = q0.astype(BF16)
    d0_ref[:, DIL_GROUP_WIDTH:] = kv0.astype(BF16)
    for w_ref, out_ref, dil in ((wd1_ref, d1_ref, DIL_PAIRS[1][1]), (wd2_ref, d2_ref, DIL_PAIRS[2][1])):
        qg, kvg = dil_proj(w_ref)
        qkv = jnp.concatenate([qg, kvg], axis=1)
        for sl in range(qkv.shape[1] // LANES):
            scr_ref[sl] = qkv[:, sl * LANES:(sl + 1) * LANES]
        for r in range(dil):
            for sl in range(qkv.shape[1] // LANES):
                out_ref[r, :, sl * LANES:(sl + 1) * LANES] = (
                    scr_ref[sl, pl.ds(r, tm // dil, stride=dil), :].astype(BF16))

    q_ref[...] = (_dot(xn, wq_ref[...]) * sb_scale).astype(BF16)
    k_ref[...] = _dot(xn, wk_ref[...]).astype(BF16)
    vt = _dot_nt(wvt_ref[...], xn).astype(BF16)
    for jb in range(tm // SB_TILE):
        vt_ref[jb] = vt[:, jb * SB_TILE:(jb + 1) * SB_TILE]


def _in_proj(x, g, wpool, wd0, wd1, wd2, wq, wk, wvt, ws5, *, tm):
    b, s, d = x.shape
    dil1, dil2 = DIL_PAIRS[1][1], DIL_PAIRS[2][1]
    gw = 3 * DIL_GROUP_WIDTH
    const = lambda shape: pl.BlockSpec(shape, lambda bi, ti: (0,) * len(shape))
    out_shape = (
        jax.ShapeDtypeStruct((b, s, POOL_WIDTH), F32),
        jax.ShapeDtypeStruct((b, s, gw), BF16),
        jax.ShapeDtypeStruct((b, dil1, s // dil1, gw), BF16),
        jax.ShapeDtypeStruct((b, dil2, s // dil2, gw), BF16),
        jax.ShapeDtypeStruct((b, s, SB_WIDTH), BF16),
        jax.ShapeDtypeStruct((b, s, SB_WIDTH), BF16),
        jax.ShapeDtypeStruct((b, s // SB_TILE, SB_WIDTH, SB_TILE), BF16),
        jax.ShapeDtypeStruct((b, s, S5_WIDTH), F32),
    )
    out_specs = (
        pl.BlockSpec((None, tm, POOL_WIDTH), lambda bi, ti: (bi, ti, 0)),
        pl.BlockSpec((None, tm, gw), lambda bi, ti: (bi, ti, 0)),
        pl.BlockSpec((None, dil1, tm // dil1, gw), lambda bi, ti: (bi, 0, ti, 0)),
        pl.BlockSpec((None, dil2, tm // dil2, gw), lambda bi, ti: (bi, 0, ti, 0)),
        pl.BlockSpec((None, tm, SB_WIDTH), lambda bi, ti: (bi, ti, 0)),
        pl.BlockSpec((None, tm, SB_WIDTH), lambda bi, ti: (bi, ti, 0)),
        pl.BlockSpec((None, tm // SB_TILE, SB_WIDTH, SB_TILE), lambda bi, ti: (bi, ti, 0, 0)),
        pl.BlockSpec((None, tm, S5_WIDTH), lambda bi, ti: (bi, ti, 0)),
    )
    in_specs = [
        pl.BlockSpec((None, tm, d), lambda bi, ti: (bi, ti, 0)),
        const(g.shape), const(wpool.shape), const(wd0.shape), const(wd1.shape), const(wd2.shape),
        const(wq.shape), const(wk.shape), const(wvt.shape), const(ws5.shape),
    ]
    sb_scale = LOG2E / math.sqrt(SB_HEAD_DIM)
    return pl.pallas_call(
        functools.partial(_in_proj_kernel, tm=tm, sb_scale=sb_scale),
        grid=(b, s // tm), in_specs=in_specs, out_specs=out_specs, out_shape=out_shape,
        scratch_shapes=[pltpu.VMEM((gw // LANES, tm, LANES), F32)],
        compiler_params=_cparams(2), name="in_proj",
    )(x, g, wpool, wd0, wd1, wd2, wq, wk, wvt, ws5)


def _dil_attn_kernel(cur_ref, halo_ref, bias_ref, o_ref, lse_ref, kv_ref, *, rows):
    nblk = rows // DIL_BAND
    gw = DIL_GROUP_WIDTH
    step = pl.program_id(2)
    kv_ref[:DIL_BAND, :] = halo_ref[:, gw:]
    kv_ref[DIL_BAND:, :] = cur_ref[:, gw:]
    lane_head = lax.broadcasted_iota(jnp.int32, (1, gw), 1) // DIL_HEAD_DIM

    def block(n, carry):
        r0 = pl.multiple_of(n * DIL_BAND, DIL_BAND)
        q = cur_ref[pl.ds(r0, DIL_BAND), :gw]
        kvb = kv_ref[pl.ds(r0, 2 * DIL_BAND), :]
        k = kvb[:, :gw]
        v = kvb[:, gw:]
        first = (step * nblk + n == 0).astype(jnp.int32)
        acc = jnp.zeros((DIL_BAND, gw), F32)
        den_b = jnp.ones((DIL_BAND, gw), F32)
        m_b = jnp.zeros((DIL_BAND, gw), F32)
        for h in range(DIL_HEADS_PER_GROUP):
            hm = lane_head == h
            hmul = hm.astype(BF16)
            s = _dot_nt(q * hmul, k) + bias_ref[first, h]
            m = jnp.max(s, axis=-1, keepdims=True)
            p = jnp.exp(s - m)
            den = jnp.sum(p, axis=-1, keepdims=True)
            acc = acc + _dot(p.astype(BF16), v * hmul)
            den_b = jnp.where(hm, den, den_b)
            m_b = jnp.where(hm, m, m_b)
        o_ref[pl.ds(r0, DIL_BAND), :] = acc / den_b
        lse_ref[pl.ds(r0, DIL_BAND), :] = m_b + jnp.log(den_b)
        return carry

    lax.fori_loop(0, nblk, block, 0)


def _dil_attn(qkv, bias):
    b, dil, length, gw3 = qkv.shape
    rows = min(length, 1024)
    rb = rows // DIL_BAND
    out_shape = (jax.ShapeDtypeStruct((b, dil, length, DIL_GROUP_WIDTH), F32),) * 2
    blk = lambda w: pl.BlockSpec((None, None, rows, w), lambda bi, ri, i: (bi, ri, i, 0))
    halo = pl.BlockSpec((None, None, DIL_BAND, gw3),
                        lambda bi, ri, i: (bi, ri, jnp.maximum(i * rb - 1, 0), 0))
    return pl.pallas_call(
        functools.partial(_dil_attn_kernel, rows=rows),
        grid=(b, dil, length // rows),
        in_specs=[blk(gw3), halo, pl.BlockSpec(bias.shape, lambda bi, ri, i: (0, 0, 0, 0))],
        out_specs=(blk(DIL_GROUP_WIDTH), blk(DIL_GROUP_WIDTH)), out_shape=out_shape,
        scratch_shapes=[pltpu.VMEM((rows + DIL_BAND, 2 * DIL_GROUP_WIDTH), BF16)],
        compiler_params=_cparams(3), name=f"dil_attn_d{dil}",
    )(qkv, qkv, bias)


def _sb_attn_kernel(q_ref, k_ref, vt_ref, tri_ref, o_ref, acc_ref):
    t = SB_TILE
    i = pl.program_id(1)
    tri = tri_ref[...]
    row = lax.broadcasted_iota(jnp.int32, (t, t), 0)
    col = lax.broadcasted_iota(jnp.int32, (t, t), 1)
    causal = row < col
    sign = jnp.uint32(0x80000000)

    heads = range(SB_HEADS)
    hsl = [slice(h * SB_HEAD_DIM, (h + 1) * SB_HEAD_DIM) for h in heads]

    def tile(j, runs, mask):
        ks = pl.multiple_of(j * t, t)
        zs = [_dot_nt(k_ref[pl.ds(ks, t), hsl[h]], q_ref[:, hsl[h]]) for h in heads]
        parts = []
        for z in zs:
            neg_abs = lax.bitcast_convert_type(lax.bitcast_convert_type(z, jnp.uint32) | sign, F32)
            sp = jnp.maximum(z, 0.0) + jnp.log(1.0 + jnp.exp2(neg_abs)) * LOG2E
            if mask is not None:
                sp = jnp.where(mask, sp, 0.0)
            hi = sp.astype(BF16)
            parts.append((hi, (sp - hi.astype(F32)).astype(BF16)))
        incls = [_dot(tri, hi) + _dot(tri, lo) for hi, lo in parts]
        ws = []
        for z, incl, run in zip(zs, incls, runs):
            w = jnp.exp2(z - incl - run)
            if mask is not None:
                w = jnp.where(mask, w, 0.0)
            ws.append(w.astype(BF16))
        pvs = [_dot(vt_ref[j, hsl[h], :], ws[h]) for h in heads]
        for h in heads:
            if mask is not None:
                acc_ref[h] = pvs[h]
            else:
                acc_ref[h] += pvs[h]
        return tuple(run + incl[0:1, :] for run, incl in zip(runs, incls))

    runs = tile(i, (jnp.zeros((1, t), F32),) * SB_HEADS, causal)
    lax.fori_loop(1, i + 1, lambda jj, r: tile(i - jj, r, None), runs)
    for h in range(SB_HEADS):
        o_ref[:, h * SB_HEAD_DIM:(h + 1) * SB_HEAD_DIM] = acc_ref[h].T.astype(o_ref.dtype)


def _sb_attn(q, k, vt, tri):
    b, s, _ = q.shape
    t = SB_TILE
    return pl.pallas_call(
        _sb_attn_kernel,
        grid=(b, s // t),
        in_specs=[
            pl.BlockSpec((None, t, SB_WIDTH), lambda bi, i: (bi, i, 0)),
            pl.BlockSpec((None, s, SB_WIDTH), lambda bi, i: (bi, 0, 0)),
            pl.BlockSpec((None, s // t, SB_WIDTH, t), lambda bi, i: (bi, 0, 0, 0)),
            pl.BlockSpec((t, t), lambda bi, i: (0, 0)),
        ],
        out_specs=pl.BlockSpec((None, t, SB_WIDTH), lambda bi, i: (bi, i, 0)),
        out_shape=jax.ShapeDtypeStruct((b, s, SB_WIDTH), BF16),
        scratch_shapes=[pltpu.VMEM((SB_HEADS, SB_HEAD_DIM, t), F32)],
        compiler_params=_cparams(2), name="sb_attn",
    )(q, k, vt, tri)


def _gelu_tanh(x):
    return 0.5 * x * (1.0 + jnp.tanh(math.sqrt(2.0 / math.pi) * (x + 0.044715 * (x * x * x))))


def _s5_kernel(u_ref, lr_ref, li_ref, wb_ref, wc_ref, d_ref, wglu_ref, out_ref, bu_ref, h_ref, *, steps):
    cs = S5_CHUNK_STATE

    @pl.when(pl.program_id(0) == 0)
    def _():
        h_ref[...] = jnp.zeros_like(h_ref)

    u = u_ref[...]
    ub = u.astype(BF16)
    for c in range(S5_CHUNKS):
        bu_ref[:, 2 * cs * c:2 * cs * (c + 1)] = _dot(ub[:, S5_CHUNK_CH * c:S5_CHUNK_CH * (c + 1)], wb_ref[c])

    nb = h_ref.shape[0]
    for c0 in range(0, S5_CHUNKS, 2):
        lam = [(jnp.broadcast_to(lr_ref[c], (nb, cs)), jnp.broadcast_to(li_ref[c], (nb, cs)))
               for c in (c0, c0 + 1)]
        offs = [2 * cs * c for c in (c0, c0 + 1)]

        def step(t, carry):
            r = pl.multiple_of(t * nb, nb)
            new = []
            for (lr, li), off, (hr, hi) in zip(lam, offs, carry):
                br = bu_ref[pl.ds(r, nb), off:off + cs]
                bi = bu_ref[pl.ds(r, nb), off + cs:off + 2 * cs]
                nhr = lr * hr - li * hi + br
                nhi = lr * hi + li * hr + bi
                bu_ref[pl.ds(r, nb), off:off + cs] = nhr
                bu_ref[pl.ds(r, nb), off + cs:off + 2 * cs] = nhi
                new.append((nhr, nhi))
            return tuple(new)

        init = tuple((h_ref[:, off:off + cs], h_ref[:, off + cs:off + 2 * cs]) for off in offs)
        fin = lax.fori_loop(0, steps, step, init)
        for off, (hr, hi) in zip(offs, fin):
            h_ref[:, off:off + cs] = hr
            h_ref[:, off + cs:off + 2 * cs] = hi

    y = jnp.concatenate(
        [_dot(bu_ref[:, 2 * cs * c:2 * cs * (c + 1)].astype(BF16), wc_ref[c]) for c in range(S5_CHUNKS)],
        axis=1)
    y = _gelu_tanh(y + d_ref[...] * u)
    gl = _dot(y.astype(BF16), wglu_ref[...])
    out_ref[...] = gl[:, :S5_WIDTH] * jax.nn.sigmoid(gl[:, S5_WIDTH:])


def _s5(u2d, lr, li, wb, wc, d, wglu, *, batch, steps):
    n, _ = u2d.shape
    rows = steps * batch
    const = lambda a: pl.BlockSpec(a.shape, lambda i: (0,) * a.ndim)
    return pl.pallas_call(
        functools.partial(_s5_kernel, steps=steps),
        grid=(n // rows,),
        in_specs=[pl.BlockSpec((rows, S5_WIDTH), lambda i: (i, 0)),
                  const(lr), const(li), const(wb), const(wc), const(d), const(wglu)],
        out_specs=pl.BlockSpec((rows, S5_WIDTH), lambda i: (i, 0)),
        out_shape=jax.ShapeDtypeStruct((n, S5_WIDTH), F32),
        scratch_shapes=[pltpu.VMEM((rows, 2 * S5_CHUNKS * S5_CHUNK_STATE), F32),
                        pltpu.VMEM((batch, 2 * S5_CHUNKS * S5_CHUNK_STATE), F32)],
        compiler_params=_cparams(1), name="s5_mixer",
    )(u2d, lr, li, wb, wc, d, wglu)


def _merge_kernel(x_ref, g_ref, up_ref, o0_ref, l0_ref, o1_ref, l1_ref, o2_ref, l2_ref, ysb_ref, ys5_ref,
                  pw_ref, ps_ref, wg_ref, wb_ref, wo_ref, out_ref, ext_ref, no1_ref, nl1_ref, no2_ref, nl2_ref,
                  *, tm):
    ti = pl.program_id(1)
    x = x_ref[...]
    xn = _rms(x, g_ref[...]).astype(BF16)

    @pl.when(ti == 0)
    def _():
        ext_ref[:POOL_HALO, :] = jnp.zeros((POOL_HALO, POOL_WIDTH), F32)

    ext_ref[POOL_HALO:, :] = up_ref[...]
    pos = ti * tm + lax.broadcasted_iota(jnp.int32, (tm, 1), 0)
    ys = []
    for gi, w in enumerate(POOL_WINDOWS):
        c0, c1 = gi * POOL_GROUP, (gi + 1) * POOL_GROUP
        acc = ext_ref[pl.ds(POOL_HALO, tm), c0:c1]
        for j in range(1, w):
            acc = acc + ext_ref[pl.ds(POOL_HALO - j, tm), c0:c1]
        cnt = jnp.minimum(pos + 1, w).astype(F32)
        p = acc / cnt - ext_ref[pl.ds(POOL_HALO, tm), c0:c1]
        ys.append(_dot(p.astype(BF16), pw_ref[gi]))
    y_pool = jnp.concatenate(ys, axis=1) * ps_ref[...]
    ext_ref[:POOL_HALO, :] = ext_ref[pl.ds(tm, POOL_HALO), :]

    nslab = DIL_GROUP_WIDTH // LANES

    def to_token_order(src_ref, dst_ref):
        dil = src_ref.shape[0]
        for r in range(dil):
            for sl in range(nslab):
                dst_ref[sl, pl.ds(r, tm // dil, stride=dil), :] = src_ref[r, :, sl * LANES:(sl + 1) * LANES]
        return jnp.concatenate([dst_ref[sl] for sl in range(nslab)], axis=1)

    o1, l1 = to_token_order(o1_ref, no1_ref), to_token_order(l1_ref, nl1_ref)
    o2, l2 = to_token_order(o2_ref, no2_ref), to_token_order(l2_ref, nl2_ref)
    l0 = l0_ref[...]
    m = jnp.maximum(jnp.maximum(l0, l1), l2)
    e0, e1, e2 = jnp.exp(l0 - m), jnp.exp(l1 - m), jnp.exp(l2 - m)
    y_dil = (e0 * o0_ref[...] + e1 * o1 + e2 * o2) / (e0 + e1 + e2)

    branches = (y_pool.astype(BF16), y_dil.astype(BF16), ysb_ref[...], ys5_ref[...].astype(BF16))
    merged = jnp.zeros((tm, x.shape[1]), F32)
    for bi, yb in enumerate(branches):
        gate = jax.nn.sigmoid(_dot(xn, wg_ref[bi]))
        merged = merged + gate * _dot(yb, wb_ref[BRANCH_ROWS[bi]:BRANCH_ROWS[bi + 1], :])
    out_ref[...] = x + _dot(merged.astype(BF16), wo_ref[...])


def _merge(x, g, up, o0, l0, o1, l1, o2, l2, ysb, ys5, pw, ps, wg, wb, wo, *, tm):
    b, s, d = x.shape
    dil1, dil2 = o1.shape[1], o2.shape[1]
    gw = DIL_GROUP_WIDTH
    tok = lambda w: pl.BlockSpec((None, tm, w), lambda bi, ti: (bi, ti, 0))
    res = lambda dil: pl.BlockSpec((None, dil, tm // dil, gw), lambda bi, ti: (bi, 0, ti, 0))
    const = lambda a: pl.BlockSpec(a.shape, lambda bi, ti: (0,) * a.ndim)
    in_specs = [tok(d), const(g), tok(POOL_WIDTH), tok(gw), tok(gw), res(dil1), res(dil1), res(dil2), res(dil2),
                tok(SB_WIDTH), tok(S5_WIDTH),
                const(pw), const(ps), const(wg), const(wb), const(wo)]
    return pl.pallas_call(
        functools.partial(_merge_kernel, tm=tm),
        grid=(b, s // tm), in_specs=in_specs, out_specs=tok(d),
        out_shape=jax.ShapeDtypeStruct((b, s, d), F32),
        scratch_shapes=[pltpu.VMEM((tm + POOL_HALO, POOL_WIDTH), F32)]
        + [pltpu.VMEM((gw // LANES, tm, LANES), F32)] * 4,
        compiler_params=_cparams(2), name="gated_merge",
    )(x, g, up, o0, l0, o1, l1, o2, l2, ysb, ys5, pw, ps, wg, wb, wo)


def _ffn_kernel(x_ref, g_ref, wup_ref, wdn_ref, fg_ref, out_ref, *, hidden, final):
    x = x_ref[...]
    hn = _rms(x, g_ref[...]).astype(BF16)
    gu = _dot(hn, wup_ref[...])
    h = (jax.nn.silu(gu[:, :hidden]) * gu[:, hidden:]).astype(BF16)
    y = x + _dot(h, wdn_ref[...])
    if final:
        y = _rms(y, fg_ref[...])
    out_ref[...] = y


def _ffn(x2d, g, wup, wdn, fg, *, tm, final):
    n, d = x2d.shape
    hidden = wdn.shape[0]
    const = lambda a: pl.BlockSpec(a.shape, lambda i: (0,) * a.ndim)
    return pl.pallas_call(
        functools.partial(_ffn_kernel, hidden=hidden, final=final),
        grid=(n // tm,),
        in_specs=[pl.BlockSpec((tm, d), lambda i: (i, 0)), const(g), const(wup), const(wdn), const(fg)],
        out_specs=pl.BlockSpec((tm, d), lambda i: (i, 0)),
        out_shape=jax.ShapeDtypeStruct((n, d), F32),
        compiler_params=_cparams(1), name="swiglu_final" if final else "swiglu",
    )(x2d, g, wup, wdn, fg)


def _t5_bucket(dist):
    exact = REL_BUCKETS // 2
    df = jnp.maximum(dist, 1).astype(F32)
    large = exact + (jnp.log(df / exact) / math.log(REL_MAX_DIST / exact)
                     * (REL_BUCKETS - exact)).astype(jnp.int32)
    large = jnp.minimum(large, REL_BUCKETS - 1)
    return jnp.where(dist < exact, dist, large)


def _dil_bias(rel_bias_g, dil):
    band = DIL_BAND
    i = jnp.arange(band)[:, None]
    c = jnp.arange(2 * band)[None, :]
    dist_sub = band + i - c
    in_band = (dist_sub >= 0) & (dist_sub <= band)
    buckets = _t5_bucket(jnp.clip(dist_sub, 0, band) * dil)
    bias = rel_bias_g.astype(F32)[buckets].transpose(2, 0, 1)
    bias = jnp.where(in_band[None], bias, NEG_INF)
    first = jnp.where((c >= band)[None], bias, NEG_INF)
    return jnp.stack([bias, first])


def _dil_weights(w_dil, group):
    hw = DIL_HEADS * DIL_HEAD_DIM
    cols = [w_dil[:, i * hw + group * DIL_GROUP_WIDTH:i * hw + (group + 1) * DIL_GROUP_WIDTH] for i in range(3)]
    return jnp.concatenate(cols, axis=1).astype(BF16)


def _s5_params(a_re, a_im, log_dt, b_re, b_im, c_re, c_im):
    lam = lax.complex(a_re.astype(F32), a_im.astype(F32))
    dt = jnp.exp(log_dt.astype(F32))[:, None]
    lam_bar = jnp.exp(lam * dt)
    b_bar = ((lam_bar - 1.0) / lam)[:, :, None] * lax.complex(b_re.astype(F32), b_im.astype(F32))
    gpc = S5_GROUPS // S5_CHUNKS
    eye = jnp.eye(gpc, dtype=F32)

    def pack_b(bm):
        bm = bm.reshape(S5_CHUNKS, gpc, S5_STATE, S5_CH)
        return jnp.einsum('cgpk,gh->cgkhp', bm, eye).reshape(S5_CHUNKS, S5_CHUNK_CH, S5_CHUNK_STATE)

    def pack_c(cm):
        cm = cm.reshape(S5_CHUNKS, gpc, S5_CH, S5_STATE)
        return jnp.einsum('cgkp,gh->cgphk', cm, eye).reshape(S5_CHUNKS, S5_CHUNK_STATE, S5_CHUNK_CH)

    wb = jnp.concatenate([pack_b(jnp.real(b_bar)), pack_b(jnp.imag(b_bar))], axis=2).astype(BF16)
    wc = jnp.concatenate([pack_c(c_re.astype(F32)), pack_c(-c_im.astype(F32))], axis=1).astype(BF16)
    lr = jnp.real(lam_bar).reshape(S5_CHUNKS, 1, S5_CHUNK_STATE)
    li = jnp.imag(lam_bar).reshape(S5_CHUNKS, 1, S5_CHUNK_STATE)
    return lr, li, wb, wc


def kernel(x, attn_norm_g, w_in, pool_w, pool_scale, rel_bias, s5_a_re, s5_a_im, s5_log_dt, s5_b_re, s5_b_im,
           s5_c_re, s5_c_im, s5_d, s5_w_glu, w_branch, w_gate, w_out, ffn_norm_g, w_up, w_down, final_norm_g):
    b, s, d = x.shape
    depth = w_in.shape[0]
    tm = 512
    o1 = POOL_WIDTH
    o2 = o1 + 3 * DIL_HEADS * DIL_HEAD_DIM
    o3 = o2 + 3 * SB_WIDTH

    t = SB_TILE
    tri = (jnp.arange(t)[None, :] >= jnp.arange(t)[:, None]).astype(BF16)
    biases = [_dil_bias(rel_bias[:, g * DIL_HEADS_PER_GROUP:(g + 1) * DIL_HEADS_PER_GROUP], dil)
              for g, (_, dil) in enumerate(DIL_PAIRS)]
    fg = final_norm_g.reshape(1, d).astype(F32)

    for l in range(depth):
        wl = w_in[l]
        g_attn = attn_norm_g[l].reshape(1, d).astype(F32)
        w_dil = wl[:, o1:o2]
        w_sb = wl[:, o2:o3]
        up, d0, d1, d2, q, k, vt, u_s5 = _in_proj(
            x, g_attn, wl[:, :o1].astype(BF16),
            _dil_weights(w_dil, 0), _dil_weights(w_dil, 1), _dil_weights(w_dil, 2),
            w_sb[:, :SB_WIDTH].astype(BF16), w_sb[:, SB_WIDTH:2 * SB_WIDTH].astype(BF16),
            w_sb[:, 2 * SB_WIDTH:].T.astype(BF16), wl[:, o3:].astype(BF16), tm=tm)

        oa0, la0 = _dil_attn(d0.reshape(b, 1, s, d0.shape[-1]), biases[0])
        oa1, la1 = _dil_attn(d1, biases[1])
        oa2, la2 = _dil_attn(d2, biases[2])
        y_sb = _sb_attn(q, k, vt, tri)

        lr, li, wb, wc = _s5_params(s5_a_re[l], s5_a_im[l], s5_log_dt[l], s5_b_re[l], s5_b_im[l],
                                    s5_c_re[l], s5_c_im[l])
        y_s5 = _s5(jnp.swapaxes(u_s5, 0, 1).reshape(s * b, S5_WIDTH), lr, li, wb, wc,
                   s5_d[l].reshape(1, S5_WIDTH).astype(F32), s5_w_glu[l].astype(BF16), batch=b, steps=64)
        y_s5 = jnp.swapaxes(y_s5.reshape(s, b, S5_WIDTH), 0, 1)

        x = _merge(x, g_attn, up, oa0.reshape(b, s, -1), la0.reshape(b, s, -1), oa1, la1, oa2, la2,
                   y_sb, y_s5,
                   pool_w[l].astype(BF16), pool_scale[l].reshape(1, POOL_WIDTH).astype(F32),
                   w_gate[l].astype(BF16), w_branch[l].astype(BF16), w_out[l].astype(BF16), tm=256)

        x = _ffn(x.reshape(b * s, d), ffn_norm_g[l].reshape(1, d).astype(F32), w_up[l].astype(BF16),
                 w_down[l].astype(BF16), fg, tm=256, final=(l == depth - 1)).reshape(b, s, d)
    return x
```

```python
import functools
import math

import jax
import jax.numpy as jnp
from jax import lax
from jax.experimental import pallas as pl
from jax.experimental.pallas import tpu as pltpu

F32 = jnp.float32
BF16 = jnp.bfloat16

EPS = 1e-6
POOL_WINDOWS = (2, 4, 8, 16)
POOL_GROUP = 128
POOL_WIDTH = 512
POOL_HALO = 16
DIL_PAIRS = ((128, 1), (512, 4), (2048, 16))
DIL_BAND = 128
DIL_HEADS_PER_GROUP = 4
DIL_HEADS = 12
DIL_HEAD_DIM = 64
DIL_GROUP_WIDTH = DIL_HEADS_PER_GROUP * DIL_HEAD_DIM
REL_BUCKETS = 32
REL_MAX_DIST = 2048
SB_HEADS = 4
SB_HEAD_DIM = 128
SB_WIDTH = 512
SB_TILE = 256
S5_WIDTH = 512
S5_CH = 16
S5_GROUPS = 32
S5_STATE = 64
S5_CHUNKS = 4
S5_CHUNK_CH = S5_WIDTH // S5_CHUNKS
S5_CHUNK_STATE = (S5_GROUPS // S5_CHUNKS) * S5_STATE
BRANCH_ROWS = (0, 512, 768, 1280, 1792)
NEG_INF = -1e30
SB_UNDERFLOW = 160.0
LANES = 128
LOG2E = math.log2(math.e)

VMEM_LIMIT = 56 * 1024 * 1024


def _cparams(n_axes):
    return pltpu.CompilerParams(dimension_semantics=("arbitrary",) * n_axes,
                                vmem_limit_bytes=VMEM_LIMIT)


def _dot(a, b):
    return jnp.dot(a, b, preferred_element_type=F32)


def _dot_nt(a, b):
    return lax.dot_general(a, b, (((1,), (1,)), ((), ())), preferred_element_type=F32)


def _rms(x, g):
    ms = jnp.mean(x * x, axis=-1, keepdims=True)
    return x * lax.rsqrt(ms + EPS) * g


def _in_proj_kernel(x_ref, g_ref, wpool_ref, wd0_ref, wd1_ref, wd2_ref, wq_ref, wk_ref, wvt_ref,
                    ws5_ref, pool_ref, d0_ref, d1_ref, d2_ref, q_ref, k_ref, vt_ref, s5_ref,
                    scr_ref, *, tm, sb_scale):
    xn = _rms(x_ref[...], g_ref[...]).astype(BF16)
    pool_ref[...] = _dot(xn, wpool_ref[...])
    s5_ref[...] = _dot(xn, ws5_ref[...])

    def dil_proj(w_ref):
        q = _dot(xn, w_ref[:, :DIL_GROUP_WIDTH]) * (1.0 / math.sqrt(DIL_HEAD_DIM))
        kv = _dot(xn, w_ref[:, DIL_GROUP_WIDTH:])
        return q, kv

    q0, kv0 = dil_proj(wd0_ref)
    d0_ref[:, :DIL_GROUP_WIDTH] = q0.astype(BF16)
    d0_ref[:, DIL_GROUP_WIDTH:] = kv0.astype(BF16)
    for w_ref, out_ref, dil in ((wd1_ref, d1_ref, DIL_PAIRS[1][1]), (wd2_ref, d2_ref, DIL_PAIRS[2][1])):
        qg, kvg = dil_proj(w_ref)
        qkv = jnp.concatenate([qg, kvg], axis=1)
        for sl in range(qkv.shape[1] // LANES):
            scr_ref[sl] = qkv[:, sl * LANES:(sl + 1) * LANES]
        for r in range(dil):
            for sl in range(qkv.shape[1] // LANES):
                out_ref[r, :, sl * LANES:(sl + 1) * LANES] = (
                    scr_ref[sl, pl.ds(r, tm // dil, stride=dil), :].astype(BF16))

    q_ref[...] = (_dot(xn, wq_ref[...]) * sb_scale).astype(BF16)
    k_ref[...] = _dot(xn, wk_ref[...]).astype(BF16)
    vt = _dot_nt(wvt_ref[...], xn).astype(BF16)
    for jb in range(tm // SB_TILE):
        vt_ref[jb] = vt[:, jb * SB_TILE:(jb + 1) * SB_TILE]


def _in_proj(x, g, wpool, wd0, wd1, wd2, wq, wk, wvt, ws5, *, tm):
    b, s, d = x.shape
    dil1, dil2 = DIL_PAIRS[1][1], DIL_PAIRS[2][1]
    gw = 3 * DIL_GROUP_WIDTH
    const = lambda shape: pl.BlockSpec(shape, lambda bi, ti: (0,) * len(shape))
    out_shape = (
        jax.ShapeDtypeStruct((b, s, POOL_WIDTH), F32),
        jax.ShapeDtypeStruct((b, s, gw), BF16),
        jax.ShapeDtypeStruct((b, dil1, s // dil1, gw), BF16),
        jax.ShapeDtypeStruct((b, dil2, s // dil2, gw), BF16),
        jax.ShapeDtypeStruct((b, s, SB_WIDTH), BF16),
        jax.ShapeDtypeStruct((b, s, SB_WIDTH), BF16),
        jax.ShapeDtypeStruct((b, s // SB_TILE, SB_WIDTH, SB_TILE), BF16),
        jax.ShapeDtypeStruct((b, s, S5_WIDTH), F32),
    )
    out_specs = (
        pl.BlockSpec((None, tm, POOL_WIDTH), lambda bi, ti: (bi, ti, 0)),
        pl.BlockSpec((None, tm, gw), lambda bi, ti: (bi, ti, 0)),
        pl.BlockSpec((None, dil1, tm // dil1, gw), lambda bi, ti: (bi, 0, ti, 0)),
        pl.BlockSpec((None, dil2, tm // dil2, gw), lambda bi, ti: (bi, 0, ti, 0)),
        pl.BlockSpec((None, tm, SB_WIDTH), lambda bi, ti: (bi, ti, 0)),
        pl.BlockSpec((None, tm, SB_WIDTH), lambda bi, ti: (bi, ti, 0)),
        pl.BlockSpec((None, tm // SB_TILE, SB_WIDTH, SB_TILE), lambda bi, ti: (bi, ti, 0, 0)),
        pl.BlockSpec((None, tm, S5_WIDTH), lambda bi, ti: (bi, ti, 0)),
    )
    in_specs = [
        pl.BlockSpec((None, tm, d), lambda bi, ti: (bi, ti, 0)),
        const(g.shape), const(wpool.shape), const(wd0.shape), const(wd1.shape), const(wd2.shape),
        const(wq.shape), const(wk.shape), const(wvt.shape), const(ws5.shape),
    ]
    sb_scale = LOG2E / math.sqrt(SB_HEAD_DIM)
    return pl.pallas_call(
        functools.partial(_in_proj_kernel, tm=tm, sb_scale=sb_scale),
        grid=(b, s // tm), in_specs=in_specs, out_specs=out_specs, out_shape=out_shape,
        scratch_shapes=[pltpu.VMEM((gw // LANES, tm, LANES), F32)],
        compiler_params=_cparams(2), name="in_proj",
    )(x, g, wpool, wd0, wd1, wd2, wq, wk, wvt, ws5)


def _dil_attn_kernel(cur_ref, halo_ref, bias_ref, ones_ref, o_ref, lse_ref, kv_ref, *, rows):
    nblk = rows // DIL_BAND
    gw = DIL_GROUP_WIDTH
    nh = DIL_HEADS_PER_GROUP
    step = pl.program_id(2)
    kv_ref[:DIL_BAND, :] = halo_ref[:, gw:]
    kv_ref[DIL_BAND:, :] = cur_ref[:, gw:]
    lane_head = lax.broadcasted_iota(jnp.int32, (1, gw), 1) // DIL_HEAD_DIM
    hms = [lane_head == h for h in range(nh)]
    hmuls = [hm.astype(BF16) for hm in hms]
    ones_bd = ones_ref[...]
    unroll = 2 if nblk % 2 == 0 else 1

    def blocks(n2, carry):
        ns = [n2 * unroll + u for u in range(unroll)]
        r0s = [pl.multiple_of(n * DIL_BAND, DIL_BAND) for n in ns]
        kvs = [kv_ref[pl.ds(r0, 2 * DIL_BAND), :] for r0 in r0s]
        ss = []
        for r0, kvb in zip(r0s, kvs):
            q = cur_ref[pl.ds(r0, DIL_BAND), :gw]
            qs = jnp.concatenate([q * hmul for hmul in hmuls], axis=0)
            ss.append(_dot_nt(qs, kvb[:, :gw]))
        ps, ms = [], []
        for n, s in zip(ns, ss):
            first = (step * nblk + n == 0).astype(jnp.int32)
            p_h, m_h = [], []
            for h in range(nh):
                sh = s[h * DIL_BAND:(h + 1) * DIL_BAND, :] + bias_ref[first, h]
                m = jnp.max(sh, axis=-1, keepdims=True)
                p_h.append(jnp.exp(sh - m).astype(BF16))
                m_h.append(m)
            ps.append(jnp.concatenate(p_h, axis=1))
            ms.append(m_h)
        accs, dens = [], []
        for p, kvb in zip(ps, kvs):
            v = kvb[:, gw:]
            v_bd = jnp.concatenate([v * hmul for hmul in hmuls], axis=0)
            accs.append(_dot(p, v_bd))
            dens.append(_dot(p, ones_bd))
        for r0, acc, den, m_h in zip(r0s, accs, dens, ms):
            m_b = m_h[0]
            for h in range(1, nh):
                m_b = jnp.where(hms[h], m_h[h], m_b)
            o_ref[pl.ds(r0, DIL_BAND), :] = acc / den
            lse_ref[pl.ds(r0, DIL_BAND), :] = m_b + jnp.log(den)
        return carry

    lax.fori_loop(0, nblk // unroll, blocks, 0)


def _dil_attn(qkv, bias, ones_bd):
    b, dil, length, gw3 = qkv.shape
    rows = min(length, 1024)
    rb = rows // DIL_BAND
    out_shape = (jax.ShapeDtypeStruct((b, dil, length, DIL_GROUP_WIDTH), F32),) * 2
    blk = lambda w: pl.BlockSpec((None, None, rows, w), lambda bi, ri, i: (bi, ri, i, 0))
    halo = pl.BlockSpec((None, None, DIL_BAND, gw3),
                        lambda bi, ri, i: (bi, ri, jnp.maximum(i * rb - 1, 0), 0))
    return pl.pallas_call(
        functools.partial(_dil_attn_kernel, rows=rows),
        grid=(b, dil, length // rows),
        in_specs=[blk(gw3), halo, pl.BlockSpec(bias.shape, lambda bi, ri, i: (0, 0, 0, 0)),
                  pl.BlockSpec(ones_bd.shape, lambda bi, ri, i: (0, 0))],
        out_specs=(blk(DIL_GROUP_WIDTH), blk(DIL_GROUP_WIDTH)), out_shape=out_shape,
        scratch_shapes=[pltpu.VMEM((rows + DIL_BAND, 2 * DIL_GROUP_WIDTH), BF16)],
        compiler_params=_cparams(3), name=f"dil_attn_d{dil}",
    )(qkv, qkv, bias, ones_bd)


def _sb_attn_kernel(q_ref, k_ref, vt_ref, ntri_ref, o_ref, acc_ref):
    tk, tq = SB_TILE, q_ref.shape[0]
    i = pl.program_id(1)
    ntri = ntri_ref[...]
    row = lax.broadcasted_iota(jnp.int32, (tk, tq), 0)
    col = lax.broadcasted_iota(jnp.int32, (tk, tq), 1)
    sign = jnp.uint32(0x80000000)
    heads = range(SB_HEADS)
    hsl = [slice(h * SB_HEAD_DIM, (h + 1) * SB_HEAD_DIM) for h in heads]
    ndiag = tq // tk

    def tile(j, runs, mask, first):
        ks = pl.multiple_of(j * tk, tk)
        kq = lambda h: _dot_nt(k_ref[pl.ds(ks, tk), hsl[h]], q_ref[:, hsl[h]])
        zs = [kq(h) for h in heads]
        his, z0s = [], []
        for z in zs:
            neg_abs = lax.bitcast_convert_type(lax.bitcast_convert_type(z, jnp.uint32) | sign, F32)
            sp = jnp.maximum(z, 0.0) + jnp.log(1.0 + jnp.exp2(neg_abs)) * LOG2E
            if mask is not None:
                sp = jnp.where(mask, sp, 0.0)
            his.append(sp.astype(BF16))
            z0s.append(z[0:1, :])
        ds = [kq(h) + _dot(ntri, his[h]) for h in heads]
        ws = []
        for d, run in zip(ds, runs):
            w = jnp.exp2(d - run)
            if mask is not None:
                w = jnp.where(mask, w, 0.0)
            ws.append(w.astype(BF16))
        pvs = [_dot(vt_ref[j, hsl[h], :], ws[h]) for h in heads]
        for h in heads:
            if first:
                acc_ref[h] = pvs[h]
            else:
                acc_ref[h] += pvs[h]
        return tuple(run + (z0 - d[0:1, :]) for run, z0, d in zip(runs, z0s, ds))

    runs = (jnp.zeros((1, tq), F32),) * SB_HEADS
    top = ndiag * i + ndiag - 1
    for dd in range(ndiag):
        runs = tile(top - dd, runs, row + (ndiag - 1 - dd) * tk < col, dd == 0)
    nfull = ndiag * i

    def min_run(runs):
        return jnp.min(jnp.minimum(jnp.minimum(runs[0], runs[1]), jnp.minimum(runs[2], runs[3])))

    def cond(carry):
        jj, mr, _ = carry
        return jnp.logical_and(jj < nfull, mr < SB_UNDERFLOW)

    def body(carry):
        jj, _, runs = carry
        runs = tile(nfull - 1 - jj, runs, None, False)
        return jj + 1, min_run(runs), runs

    lax.while_loop(cond, body, (jnp.int32(0), min_run(runs), runs))
    for h in heads:
        o_ref[:, hsl[h]] = acc_ref[h].T.astype(o_ref.dtype)


def _sb_attn(q, k, vt, ntri, *, tq):
    b, s, _ = q.shape
    tk = SB_TILE
    return pl.pallas_call(
        _sb_attn_kernel,
        grid=(b, s // tq),
        in_specs=[
            pl.BlockSpec((None, tq, SB_WIDTH), lambda bi, i: (bi, i, 0)),
            pl.BlockSpec((None, s, SB_WIDTH), lambda bi, i: (bi, 0, 0)),
            pl.BlockSpec((None, s // tk, SB_WIDTH, tk), lambda bi, i: (bi, 0, 0, 0)),
            pl.BlockSpec((tk, tk), lambda bi, i: (0, 0)),
        ],
        out_specs=pl.BlockSpec((None, tq, SB_WIDTH), lambda bi, i: (bi, i, 0)),
        out_shape=jax.ShapeDtypeStruct((b, s, SB_WIDTH), BF16),
        scratch_shapes=[pltpu.VMEM((SB_HEADS, SB_HEAD_DIM, tq), F32)],
        compiler_params=_cparams(2), name="sb_attn",
    )(q, k, vt, ntri)


def _gelu_tanh(x):
    return 0.5 * x * (1.0 + jnp.tanh(math.sqrt(2.0 / math.pi) * (x + 0.044715 * (x * x * x))))


def _s5_kernel(u_ref, lr_ref, li_ref, wb_ref, wc_ref, d_ref, wglu_ref, out_ref, bu_ref, h_ref, *, steps):
    cs = S5_CHUNK_STATE

    @pl.when(pl.program_id(0) == 0)
    def _():
        h_ref[...] = jnp.zeros_like(h_ref)

    u = u_ref[...]
    ub = u.astype(BF16)
    for c in range(S5_CHUNKS):
        bu_ref[:, 2 * cs * c:2 * cs * (c + 1)] = _dot(ub[:, S5_CHUNK_CH * c:S5_CHUNK_CH * (c + 1)], wb_ref[c])

    nb = h_ref.shape[0]
    for c0 in range(0, S5_CHUNKS, 2):
        lam = [(jnp.broadcast_to(lr_ref[c], (nb, cs)), jnp.broadcast_to(li_ref[c], (nb, cs)))
               for c in (c0, c0 + 1)]
        offs = [2 * cs * c for c in (c0, c0 + 1)]

        def step(t, carry):
            r = pl.multiple_of(t * nb, nb)
            new = []
            for (lr, li), off, (hr, hi) in zip(lam, offs, carry):
                br = bu_ref[pl.ds(r, nb), off:off + cs]
                bi = bu_ref[pl.ds(r, nb), off + cs:off + 2 * cs]
                nhr = lr * hr - li * hi + br
                nhi = lr * hi + li * hr + bi
                bu_ref[pl.ds(r, nb), off:off + cs] = nhr
                bu_ref[pl.ds(r, nb), off + cs:off + 2 * cs] = nhi
                new.append((nhr, nhi))
            return tuple(new)

        init = tuple((h_ref[:, off:off + cs], h_ref[:, off + cs:off + 2 * cs]) for off in offs)
        fin = lax.fori_loop(0, steps, step, init)
        for off, (hr, hi) in zip(offs, fin):
            h_ref[:, off:off + cs] = hr
            h_ref[:, off + cs:off + 2 * cs] = hi

    y = jnp.concatenate(
        [_dot(bu_ref[:, 2 * cs * c:2 * cs * (c + 1)].astype(BF16), wc_ref[c]) for c in range(S5_CHUNKS)],
        axis=1)
    y = _gelu_tanh(y + d_ref[...] * u)
    gl = _dot(y.astype(BF16), wglu_ref[...])
    out_ref[...] = gl[:, :S5_WIDTH] * jax.nn.sigmoid(gl[:, S5_WIDTH:])


def _s5(u2d, lr, li, wb, wc, d, wglu, *, batch, steps):
    n, _ = u2d.shape
    rows = steps * batch
    const = lambda a: pl.BlockSpec(a.shape, lambda i: (0,) * a.ndim)
    return pl.pallas_call(
        functools.partial(_s5_kernel, steps=steps),
        grid=(n // rows,),
        in_specs=[pl.BlockSpec((rows, S5_WIDTH), lambda i: (i, 0)),
                  const(lr), const(li), const(wb), const(wc), const(d), const(wglu)],
        out_specs=pl.BlockSpec((rows, S5_WIDTH), lambda i: (i, 0)),
        out_shape=jax.ShapeDtypeStruct((n, S5_WIDTH), F32),
        scratch_shapes=[pltpu.VMEM((rows, 2 * S5_CHUNKS * S5_CHUNK_STATE), F32),
                        pltpu.VMEM((batch, 2 * S5_CHUNKS * S5_CHUNK_STATE), F32)],
        compiler_params=_cparams(1), name="s5_mixer",
    )(u2d, lr, li, wb, wc, d, wglu)


def _merge_kernel(x_ref, g_ref, up_ref, o0_ref, l0_ref, o1_ref, l1_ref, o2_ref, l2_ref, ysb_ref, ys5_ref,
                  pw_ref, ps_ref, wg_ref, wb_ref, wo_ref, out_ref, ext_ref, no1_ref, nl1_ref, no2_ref, nl2_ref,
                  *, tm):
    ti = pl.program_id(1)
    x = x_ref[...]
    xn = _rms(x, g_ref[...]).astype(BF16)

    @pl.when(ti == 0)
    def _():
        ext_ref[:POOL_HALO, :] = jnp.zeros((POOL_HALO, POOL_WIDTH), F32)

    ext_ref[POOL_HALO:, :] = up_ref[...]
    pos = ti * tm + lax.broadcasted_iota(jnp.int32, (tm, 1), 0)
    ys = []
    for gi, w in enumerate(POOL_WINDOWS):
        c0, c1 = gi * POOL_GROUP, (gi + 1) * POOL_GROUP
        acc = ext_ref[pl.ds(POOL_HALO, tm), c0:c1]
        for j in range(1, w):
            acc = acc + ext_ref[pl.ds(POOL_HALO - j, tm), c0:c1]
        cnt = jnp.minimum(pos + 1, w).astype(F32)
        p = acc / cnt - ext_ref[pl.ds(POOL_HALO, tm), c0:c1]
        ys.append(_dot(p.astype(BF16), pw_ref[gi]))
    y_pool = jnp.concatenate(ys, axis=1) * ps_ref[...]
    ext_ref[:POOL_HALO, :] = ext_ref[pl.ds(tm, POOL_HALO), :]

    nslab = DIL_GROUP_WIDTH // LANES

    def to_token_order(src_ref, dst_ref):
        dil = src_ref.shape[0]
        for r in range(dil):
            for sl in range(nslab):
                dst_ref[sl, pl.ds(r, tm // dil, stride=dil), :] = src_ref[r, :, sl * LANES:(sl + 1) * LANES]
        return jnp.concatenate([dst_ref[sl] for sl in range(nslab)], axis=1)

    o1, l1 = to_token_order(o1_ref, no1_ref), to_token_order(l1_ref, nl1_ref)
    o2, l2 = to_token_order(o2_ref, no2_ref), to_token_order(l2_ref, nl2_ref)
    l0 = l0_ref[...]
    m = jnp.maximum(jnp.maximum(l0, l1), l2)
    e0, e1, e2 = jnp.exp(l0 - m), jnp.exp(l1 - m), jnp.exp(l2 - m)
    y_dil = (e0 * o0_ref[...] + e1 * o1 + e2 * o2) / (e0 + e1 + e2)

    branches = (y_pool.astype(BF16), y_dil.astype(BF16), ysb_ref[...], ys5_ref[...].astype(BF16))
    merged = jnp.zeros((tm, x.shape[1]), F32)
    for bi, yb in enumerate(branches):
        gate = jax.nn.sigmoid(_dot(xn, wg_ref[bi]))
        merged = merged + gate * _dot(yb, wb_ref[BRANCH_ROWS[bi]:BRANCH_ROWS[bi + 1], :])
    out_ref[...] = x + _dot(merged.astype(BF16), wo_ref[...])


def _merge(x, g, up, o0, l0, o1, l1, o2, l2, ysb, ys5, pw, ps, wg, wb, wo, *, tm):
    b, s, d = x.shape
    dil1, dil2 = o1.shape[1], o2.shape[1]
    gw = DIL_GROUP_WIDTH
    tok = lambda w: pl.BlockSpec((None, tm, w), lambda bi, ti: (bi, ti, 0))
    res = lambda dil: pl.BlockSpec((None, dil, tm // dil, gw), lambda bi, ti: (bi, 0, ti, 0))
    const = lambda a: pl.BlockSpec(a.shape, lambda bi, ti: (0,) * a.ndim)
    in_specs = [tok(d), const(g), tok(POOL_WIDTH), tok(gw), tok(gw), res(dil1), res(dil1), res(dil2), res(dil2),
                tok(SB_WIDTH), tok(S5_WIDTH),
                const(pw), const(ps), const(wg), const(wb), const(wo)]
    return pl.pallas_call(
        functools.partial(_merge_kernel, tm=tm),
        grid=(b, s // tm), in_specs=in_specs, out_specs=tok(d),
        out_shape=jax.ShapeDtypeStruct((b, s, d), F32),
        scratch_shapes=[pltpu.VMEM((tm + POOL_HALO, POOL_WIDTH), F32)]
        + [pltpu.VMEM((gw // LANES, tm, LANES), F32)] * 4,
        compiler_params=_cparams(2), name="gated_merge",
    )(x, g, up, o0, l0, o1, l1, o2, l2, ysb, ys5, pw, ps, wg, wb, wo)


def _ffn_kernel(x_ref, g_ref, wup_ref, wdn_ref, fg_ref, out_ref, *, hidden, final):
    x = x_ref[...]
    hn = _rms(x, g_ref[...]).astype(BF16)
    gu = _dot(hn, wup_ref[...])
    h = (jax.nn.silu(gu[:, :hidden]) * gu[:, hidden:]).astype(BF16)
    y = x + _dot(h, wdn_ref[...])
    if final:
        y = _rms(y, fg_ref[...])
    out_ref[...] = y


def _ffn(x2d, g, wup, wdn, fg, *, tm, final):
    n, d = x2d.shape
    hidden = wdn.shape[0]
    const = lambda a: pl.BlockSpec(a.shape, lambda i: (0,) * a.ndim)
    return pl.pallas_call(
        functools.partial(_ffn_kernel, hidden=hidden, final=final),
        grid=(n // tm,),
        in_specs=[pl.BlockSpec((tm, d), lambda i: (i, 0)), const(g), const(wup), const(wdn), const(fg)],
        out_specs=pl.BlockSpec((tm, d), lambda i: (i, 0)),
        out_shape=jax.ShapeDtypeStruct((n, d), F32),
        compiler_params=_cparams(1), name="swiglu_final" if final else "swiglu",
    )(x2d, g, wup, wdn, fg)


def _t5_bucket(dist):
    exact = REL_BUCKETS // 2
    df = jnp.maximum(dist, 1).astype(F32)
    large = exact + (jnp.log(df / exact) / math.log(REL_MAX_DIST / exact)
                     * (REL_BUCKETS - exact)).astype(jnp.int32)
    large = jnp.minimum(large, REL_BUCKETS - 1)
    return jnp.where(dist < exact, dist, large)


def _dil_bias(rel_bias_g, dil):
    band = DIL_BAND
    i = jnp.arange(band)[:, None]
    c = jnp.arange(2 * band)[None, :]
    dist_sub = band + i - c
    in_band = (dist_sub >= 0) & (dist_sub <= band)
    buckets = _t5_bucket(jnp.clip(dist_sub, 0, band) * dil)
    onehot = (buckets[None] == jnp.arange(REL_BUCKETS)[:, None, None]).astype(F32)
    bias = jnp.einsum('kh,kic->hic', rel_bias_g.astype(F32), onehot, precision=lax.Precision.HIGHEST)
    bias = jnp.where(in_band[None], bias, NEG_INF)
    first = jnp.where((c >= band)[None], bias, NEG_INF)
    return jnp.stack([bias, first])


def _dil_weights(w_dil, group):
    hw = DIL_HEADS * DIL_HEAD_DIM
    cols = [w_dil[:, i * hw + group * DIL_GROUP_WIDTH:i * hw + (group + 1) * DIL_GROUP_WIDTH] for i in range(3)]
    return jnp.concatenate(cols, axis=1).astype(BF16)


def _s5_params(a_re, a_im, log_dt, b_re, b_im, c_re, c_im):
    lam = lax.complex(a_re.astype(F32), a_im.astype(F32))
    dt = jnp.exp(log_dt.astype(F32))[:, None]
    lam_bar = jnp.exp(lam * dt)
    b_bar = ((lam_bar - 1.0) / lam)[:, :, None] * lax.complex(b_re.astype(F32), b_im.astype(F32))
    gpc = S5_GROUPS // S5_CHUNKS
    eye = jnp.eye(gpc, dtype=F32)

    def pack_b(bm):
        bm = bm.reshape(S5_CHUNKS, gpc, S5_STATE, S5_CH)
        return jnp.einsum('cgpk,gh->cgkhp', bm, eye).reshape(S5_CHUNKS, S5_CHUNK_CH, S5_CHUNK_STATE)

    def pack_c(cm):
        cm = cm.reshape(S5_CHUNKS, gpc, S5_CH, S5_STATE)
        return jnp.einsum('cgkp,gh->cgphk', cm, eye).reshape(S5_CHUNKS, S5_CHUNK_STATE, S5_CHUNK_CH)

    wb = jnp.concatenate([pack_b(jnp.real(b_bar)), pack_b(jnp.imag(b_bar))], axis=2).astype(BF16)
    wc = jnp.concatenate([pack_c(c_re.astype(F32)), pack_c(-c_im.astype(F32))], axis=1).astype(BF16)
    lr = jnp.real(lam_bar).reshape(S5_CHUNKS, 1, S5_CHUNK_STATE)
    li = jnp.imag(lam_bar).reshape(S5_CHUNKS, 1, S5_CHUNK_STATE)
    return lr, li, wb, wc


def kernel(x, attn_norm_g, w_in, pool_w, pool_scale, rel_bias, s5_a_re, s5_a_im, s5_log_dt, s5_b_re, s5_b_im,
           s5_c_re, s5_c_im, s5_d, s5_w_glu, w_branch, w_gate, w_out, ffn_norm_g, w_up, w_down, final_norm_g):
    b, s, d = x.shape
    depth = w_in.shape[0]
    tm = 512
    o1 = POOL_WIDTH
    o2 = o1 + 3 * DIL_HEADS * DIL_HEAD_DIM
    o3 = o2 + 3 * SB_WIDTH

    t = SB_TILE
    ntri = -(jnp.arange(t)[None, :] >= jnp.arange(t)[:, None]).astype(BF16)
    biases = [_dil_bias(rel_bias[:, g * DIL_HEADS_PER_GROUP:(g + 1) * DIL_HEADS_PER_GROUP], dil)
              for g, (_, dil) in enumerate(DIL_PAIRS)]
    key_head = jnp.arange(DIL_HEADS_PER_GROUP * 2 * DIL_BAND) // (2 * DIL_BAND)
    ones_bd = (key_head[:, None] == (jnp.arange(DIL_GROUP_WIDTH) // DIL_HEAD_DIM)[None, :]).astype(BF16)
    fg = final_norm_g.reshape(1, d).astype(F32)

    for l in range(depth):
        wl = w_in[l]
        g_attn = attn_norm_g[l].reshape(1, d).astype(F32)
        w_dil = wl[:, o1:o2]
        w_sb = wl[:, o2:o3]
        up, d0, d1, d2, q, k, vt, u_s5 = _in_proj(
            x, g_attn, wl[:, :o1].astype(BF16),
            _dil_weights(w_dil, 0), _dil_weights(w_dil, 1), _dil_weights(w_dil, 2),
            w_sb[:, :SB_WIDTH].astype(BF16), w_sb[:, SB_WIDTH:2 * SB_WIDTH].astype(BF16),
            w_sb[:, 2 * SB_WIDTH:].T.astype(BF16), wl[:, o3:].astype(BF16), tm=tm)

        oa0, la0 = _dil_attn(d0.reshape(b, 1, s, d0.shape[-1]), biases[0], ones_bd)
        oa1, la1 = _dil_attn(d1, biases[1], ones_bd)
        oa2, la2 = _dil_attn(d2, biases[2], ones_bd)
        y_sb = _sb_attn(q, k, vt, ntri, tq=256)

        lr, li, wb, wc = _s5_params(s5_a_re[l], s5_a_im[l], s5_log_dt[l], s5_b_re[l], s5_b_im[l],
                                    s5_c_re[l], s5_c_im[l])
        y_s5 = _s5(jnp.swapaxes(u_s5, 0, 1).reshape(s * b, S5_WIDTH), lr, li, wb, wc,
                   s5_d[l].reshape(1, S5_WIDTH).astype(F32), s5_w_glu[l].astype(BF16), batch=b, steps=64)
        y_s5 = jnp.swapaxes(y_s5.reshape(s, b, S5_WIDTH), 0, 1)

        x = _merge(x, g_attn, up, oa0.reshape(b, s, -1), la0.reshape(b, s, -1), oa1, la1, oa2, la2,
                   y_sb, y_s5,
                   pool_w[l].astype(BF16), pool_scale[l].reshape(1, POOL_WIDTH).astype(F32),
                   w_gate[l].astype(BF16), w_branch[l].astype(BF16), w_out[l].astype(BF16), tm=256)

        x = _ffn(x.reshape(b * s, d), ffn_norm_g[l].reshape(1, d).astype(F32), w_up[l].astype(BF16),
                 w_down[l].astype(BF16), fg, tm=256, final=(l == depth - 1)).reshape(b, s, d)
    return x
```

```python
import functools
import math

import jax
import jax.numpy as jnp
from jax import lax
from jax.experimental import pallas as pl
from jax.experimental.pallas import tpu as pltpu

F32 = jnp.float32
BF16 = jnp.bfloat16

EPS = 1e-6
POOL_WINDOWS = (2, 4, 8, 16)
POOL_GROUP = 128
POOL_WIDTH = 512
POOL_HALO = 16
DIL_PAIRS = ((128, 1), (512, 4), (2048, 16))
DIL_BAND = 128
DIL_HEADS_PER_GROUP = 4
DIL_HEADS = 12
DIL_HEAD_DIM = 64
DIL_GROUP_WIDTH = DIL_HEADS_PER_GROUP * DIL_HEAD_DIM
REL_BUCKETS = 32
REL_MAX_DIST = 2048
SB_HEADS = 4
SB_HEAD_DIM = 128
SB_WIDTH = 512
SB_TILE = 256
S5_WIDTH = 512
S5_CH = 16
S5_GROUPS = 32
S5_STATE = 64
S5_CHUNKS = 4
S5_CHUNK_CH = S5_WIDTH // S5_CHUNKS
S5_CHUNK_STATE = (S5_GROUPS // S5_CHUNKS) * S5_STATE
BRANCH_ROWS = (0, 512, 768, 1280, 1792)
NEG_INF = -1e30
SB_UNDERFLOW = 160.0
LANES = 128
LOG2E = math.log2(math.e)

VMEM_LIMIT = 56 * 1024 * 1024


def _cparams(n_axes):
    return pltpu.CompilerParams(dimension_semantics=("arbitrary",) * n_axes,
                                vmem_limit_bytes=VMEM_LIMIT)


def _dot(a, b):
    return jnp.dot(a, b, preferred_element_type=F32)


def _dot_nt(a, b):
    return lax.dot_general(a, b, (((1,), (1,)), ((), ())), preferred_element_type=F32)


def _rms(x, g):
    ms = jnp.mean(x * x, axis=-1, keepdims=True)
    return x * lax.rsqrt(ms + EPS) * g


def _in_proj_kernel(x_ref, g_ref, wpool_ref, wd0_ref, wd1_ref, wd2_ref, wq_ref, wk_ref, wvt_ref,
                    ws5_ref, pool_ref, d0_ref, d1_ref, d2_ref, q_ref, k_ref, vt_ref, s5_ref,
                    scr_ref, *, tm, sb_scale):
    xn = _rms(x_ref[...], g_ref[...]).astype(BF16)
    pool_ref[...] = _dot(xn, wpool_ref[...])
    s5_ref[...] = _dot(xn, ws5_ref[...])

    def dil_proj(w_ref):
        q = _dot(xn, w_ref[:, :DIL_GROUP_WIDTH]) * (1.0 / math.sqrt(DIL_HEAD_DIM))
        kv = _dot(xn, w_ref[:, DIL_GROUP_WIDTH:])
        return q, kv

    q0, kv0 = dil_proj(wd0_ref)
    d0_ref[:, :DIL_GROUP_WIDTH] = q0.astype(BF16)
    d0_ref[:, DIL_GROUP_WIDTH:] = kv0.astype(BF16)
    for w_ref, out_ref, dil in ((wd1_ref, d1_ref, DIL_PAIRS[1][1]), (wd2_ref, d2_ref, DIL_PAIRS[2][1])):
        qg, kvg = dil_proj(w_ref)
        qkv = jnp.concatenate([qg, kvg], axis=1)
        for sl in range(qkv.shape[1] // LANES):
            scr_ref[sl] = qkv[:, sl * LANES:(sl + 1) * LANES]
        for r in range(dil):
            for sl in range(qkv.shape[1] // LANES):
                out_ref[r, :, sl * LANES:(sl + 1) * LANES] = (
                    scr_ref[sl, pl.ds(r, tm // dil, stride=dil), :].astype(BF16))

    q_ref[...] = (_dot(xn, wq_ref[...]) * sb_scale).astype(BF16)
    k_ref[...] = _dot(xn, wk_ref[...]).astype(BF16)
    vt = _dot_nt(wvt_ref[...], xn).astype(BF16)
    for jb in range(tm // SB_TILE):
        vt_ref[jb] = vt[:, jb * SB_TILE:(jb + 1) * SB_TILE]


def _in_proj(x, g, wpool, wd0, wd1, wd2, wq, wk, wvt, ws5, *, tm):
    b, s, d = x.shape
    dil1, dil2 = DIL_PAIRS[1][1], DIL_PAIRS[2][1]
    gw = 3 * DIL_GROUP_WIDTH
    const = lambda shape: pl.BlockSpec(shape, lambda bi, ti: (0,) * len(shape))
    out_shape = (
        jax.ShapeDtypeStruct((b, s, POOL_WIDTH), F32),
        jax.ShapeDtypeStruct((b, s, gw), BF16),
        jax.ShapeDtypeStruct((b, dil1, s // dil1, gw), BF16),
        jax.ShapeDtypeStruct((b, dil2, s // dil2, gw), BF16),
        jax.ShapeDtypeStruct((b, s, SB_WIDTH), BF16),
        jax.ShapeDtypeStruct((b, s, SB_WIDTH), BF16),
        jax.ShapeDtypeStruct((b, s // SB_TILE, SB_WIDTH, SB_TILE), BF16),
        jax.ShapeDtypeStruct((b, s, S5_WIDTH), F32),
    )
    out_specs = (
        pl.BlockSpec((None, tm, POOL_WIDTH), lambda bi, ti: (bi, ti, 0)),
        pl.BlockSpec((None, tm, gw), lambda bi, ti: (bi, ti, 0)),
        pl.BlockSpec((None, dil1, tm // dil1, gw), lambda bi, ti: (bi, 0, ti, 0)),
        pl.BlockSpec((None, dil2, tm // dil2, gw), lambda bi, ti: (bi, 0, ti, 0)),
        pl.BlockSpec((None, tm, SB_WIDTH), lambda bi, ti: (bi, ti, 0)),
        pl.BlockSpec((None, tm, SB_WIDTH), lambda bi, ti: (bi, ti, 0)),
        pl.BlockSpec((None, tm // SB_TILE, SB_WIDTH, SB_TILE), lambda bi, ti: (bi, ti, 0, 0)),
        pl.BlockSpec((None, tm, S5_WIDTH), lambda bi, ti: (bi, ti, 0)),
    )
    in_specs = [
        pl.BlockSpec((None, tm, d), lambda bi, ti: (bi, ti, 0)),
        const(g.shape), const(wpool.shape), const(wd0.shape), const(wd1.shape), const(wd2.shape),
        const(wq.shape), const(wk.shape), const(wvt.shape), const(ws5.shape),
    ]
    sb_scale = LOG2E / math.sqrt(SB_HEAD_DIM)
    return pl.pallas_call(
        functools.partial(_in_proj_kernel, tm=tm, sb_scale=sb_scale),
        grid=(b, s // tm), in_specs=in_specs, out_specs=out_specs, out_shape=out_shape,
        scratch_shapes=[pltpu.VMEM((gw // LANES, tm, LANES), F32)],
        compiler_params=_cparams(2), name="in_proj",
    )(x, g, wpool, wd0, wd1, wd2, wq, wk, wvt, ws5)


def _dil_attn_kernel(cur_ref, halo_ref, bias_ref, ones_ref, o_ref, lse_ref, kv_ref, *, rows):
    nblk = rows // DIL_BAND
    gw = DIL_GROUP_WIDTH
    nh = DIL_HEADS_PER_GROUP
    step = pl.program_id(2)
    kv_ref[:DIL_BAND, :] = halo_ref[:, gw:]
    kv_ref[DIL_BAND:, :] = cur_ref[:, gw:]
    lane_head = lax.broadcasted_iota(jnp.int32, (1, gw), 1) // DIL_HEAD_DIM
    hms = [lane_head == h for h in range(nh)]
    hmuls = [hm.astype(BF16) for hm in hms]
    ones_bd = ones_ref[...]
    unroll = 4 if nblk % 4 == 0 else (2 if nblk % 2 == 0 else 1)

    def blocks(n2, carry):
        ns = [n2 * unroll + u for u in range(unroll)]
        r0s = [pl.multiple_of(n * DIL_BAND, DIL_BAND) for n in ns]
        kvs = [kv_ref[pl.ds(r0, 2 * DIL_BAND), :] for r0 in r0s]
        ss = []
        for r0, kvb in zip(r0s, kvs):
            q = cur_ref[pl.ds(r0, DIL_BAND), :gw]
            qs = jnp.concatenate([q * hmul for hmul in hmuls], axis=0)
            ss.append(_dot_nt(qs, kvb[:, :gw]))
        ps, ms = [], []
        for n, s in zip(ns, ss):
            first = (step * nblk + n == 0).astype(jnp.int32)
            p_h, m_h = [], []
            for h in range(nh):
                sh = s[h * DIL_BAND:(h + 1) * DIL_BAND, :] + bias_ref[first, h]
                m = jnp.max(sh, axis=-1, keepdims=True)
                p_h.append(jnp.exp(sh - m).astype(BF16))
                m_h.append(m)
            ps.append(jnp.concatenate(p_h, axis=1))
            ms.append(m_h)
        accs, dens = [], []
        for p, kvb in zip(ps, kvs):
            v = kvb[:, gw:]
            v_bd = jnp.concatenate([v * hmul for hmul in hmuls], axis=0)
            accs.append(_dot(p, v_bd))
            dens.append(_dot(p, ones_bd))
        for r0, acc, den, m_h in zip(r0s, accs, dens, ms):
            m_b = m_h[0]
            for h in range(1, nh):
                m_b = jnp.where(hms[h], m_h[h], m_b)
            o_ref[pl.ds(r0, DIL_BAND), :] = acc / den
            lse_ref[pl.ds(r0, DIL_BAND), :] = m_b + jnp.log(den)
        return carry

    lax.fori_loop(0, nblk // unroll, blocks, 0)


def _dil_attn(qkv, bias, ones_bd):
    b, dil, length, gw3 = qkv.shape
    rows = min(length, 1024)
    rb = rows // DIL_BAND
    out_shape = (jax.ShapeDtypeStruct((b, dil, length, DIL_GROUP_WIDTH), F32),) * 2
    blk = lambda w: pl.BlockSpec((None, None, rows, w), lambda bi, ri, i: (bi, ri, i, 0))
    halo = pl.BlockSpec((None, None, DIL_BAND, gw3),
                        lambda bi, ri, i: (bi, ri, jnp.maximum(i * rb - 1, 0), 0))
    return pl.pallas_call(
        functools.partial(_dil_attn_kernel, rows=rows),
        grid=(b, dil, length // rows),
        in_specs=[blk(gw3), halo, pl.BlockSpec(bias.shape, lambda bi, ri, i: (0, 0, 0, 0)),
                  pl.BlockSpec(ones_bd.shape, lambda bi, ri, i: (0, 0))],
        out_specs=(blk(DIL_GROUP_WIDTH), blk(DIL_GROUP_WIDTH)), out_shape=out_shape,
        scratch_shapes=[pltpu.VMEM((rows + DIL_BAND, 2 * DIL_GROUP_WIDTH), BF16)],
        compiler_params=_cparams(3), name=f"dil_attn_d{dil}",
    )(qkv, qkv, bias, ones_bd)


def _sb_attn_kernel(q_ref, k_ref, vt_ref, ntri_ref, o_ref, acc_ref):
    tk, tq = SB_TILE, q_ref.shape[0]
    i = pl.program_id(1)
    ntri = ntri_ref[...]
    row = lax.broadcasted_iota(jnp.int32, (tk, tq), 0)
    col = lax.broadcasted_iota(jnp.int32, (tk, tq), 1)
    sign = jnp.uint32(0x80000000)
    heads = range(SB_HEADS)
    hsl = [slice(h * SB_HEAD_DIM, (h + 1) * SB_HEAD_DIM) for h in heads]
    ndiag = tq // tk

    def tile(j, runs, mask, first):
        ks = pl.multiple_of(j * tk, tk)
        kq = lambda h: _dot_nt(k_ref[pl.ds(ks, tk), hsl[h]], q_ref[:, hsl[h]])
        zs = [kq(h) for h in heads]
        his, z0s = [], []
        for z in zs:
            neg_abs = lax.bitcast_convert_type(lax.bitcast_convert_type(z, jnp.uint32) | sign, F32)
            sp = jnp.maximum(z, 0.0) + jnp.log(1.0 + jnp.exp2(neg_abs)) * LOG2E
            if mask is not None:
                sp = jnp.where(mask, sp, 0.0)
            his.append(sp.astype(BF16))
            z0s.append(z[0:1, :])
        ds = [kq(h) + _dot(ntri, his[h]) for h in heads]
        ws = []
        for d, run in zip(ds, runs):
            w = jnp.exp2(d - run)
            if mask is not None:
                w = jnp.where(mask, w, 0.0)
            ws.append(w.astype(BF16))
        pvs = [_dot(vt_ref[j, hsl[h], :], ws[h]) for h in heads]
        for h in heads:
            if first:
                acc_ref[h] = pvs[h]
            else:
                acc_ref[h] += pvs[h]
        return tuple(run + (z0 - d[0:1, :]) for run, z0, d in zip(runs, z0s, ds))

    runs = (jnp.zeros((1, tq), F32),) * SB_HEADS
    top = ndiag * i + ndiag - 1
    for dd in range(ndiag):
        runs = tile(top - dd, runs, row + (ndiag - 1 - dd) * tk < col, dd == 0)
    nfull = ndiag * i

    def min_run(runs):
        return jnp.min(jnp.minimum(jnp.minimum(runs[0], runs[1]), jnp.minimum(runs[2], runs[3])))

    def cond(carry):
        jj, mr, _ = carry
        return jnp.logical_and(jj < nfull, mr < SB_UNDERFLOW)

    def body(carry):
        jj, _, runs = carry
        runs = tile(nfull - 1 - jj, runs, None, False)
        return jj + 1, min_run(runs), runs

    lax.while_loop(cond, body, (jnp.int32(0), min_run(runs), runs))
    for h in heads:
        o_ref[:, hsl[h]] = acc_ref[h].T.astype(o_ref.dtype)


def _sb_attn(q, k, vt, ntri, *, tq):
    b, s, _ = q.shape
    tk = SB_TILE
    return pl.pallas_call(
        _sb_attn_kernel,
        grid=(b, s // tq),
        in_specs=[
            pl.BlockSpec((None, tq, SB_WIDTH), lambda bi, i: (bi, i, 0)),
            pl.BlockSpec((None, s, SB_WIDTH), lambda bi, i: (bi, 0, 0)),
            pl.BlockSpec((None, s // tk, SB_WIDTH, tk), lambda bi, i: (bi, 0, 0, 0)),
            pl.BlockSpec((tk, tk), lambda bi, i: (0, 0)),
        ],
        out_specs=pl.BlockSpec((None, tq, SB_WIDTH), lambda bi, i: (bi, i, 0)),
        out_shape=jax.ShapeDtypeStruct((b, s, SB_WIDTH), BF16),
        scratch_shapes=[pltpu.VMEM((SB_HEADS, SB_HEAD_DIM, tq), F32)],
        compiler_params=_cparams(2), name="sb_attn",
    )(q, k, vt, ntri)


def _gelu_tanh(x):
    return 0.5 * x * (1.0 + jnp.tanh(math.sqrt(2.0 / math.pi) * (x + 0.044715 * (x * x * x))))


def _s5_kernel(u_ref, lr_ref, li_ref, wb_ref, wc_ref, d_ref, wglu_ref, out_ref, bu_ref, h_ref, *, steps):
    cs = S5_CHUNK_STATE

    @pl.when(pl.program_id(0) == 0)
    def _():
        h_ref[...] = jnp.zeros_like(h_ref)

    nb = h_ref.shape[0]
    u = pltpu.einshape("btc->tbc", u_ref[...]).reshape(steps * nb, S5_WIDTH)
    ub = u.astype(BF16)
    for c in range(S5_CHUNKS):
        bu_ref[:, 2 * cs * c:2 * cs * (c + 1)] = _dot(ub[:, S5_CHUNK_CH * c:S5_CHUNK_CH * (c + 1)], wb_ref[c])

    for c0 in range(0, S5_CHUNKS, 2):
        lam = [(jnp.broadcast_to(lr_ref[c], (nb, cs)), jnp.broadcast_to(li_ref[c], (nb, cs)))
               for c in (c0, c0 + 1)]
        offs = [2 * cs * c for c in (c0, c0 + 1)]

        def step(t, carry):
            r = pl.multiple_of(t * nb, nb)
            new = []
            for (lr, li), off, (hr, hi) in zip(lam, offs, carry):
                br = bu_ref[pl.ds(r, nb), off:off + cs]
                bi = bu_ref[pl.ds(r, nb), off + cs:off + 2 * cs]
                nhr = lr * hr - li * hi + br
                nhi = lr * hi + li * hr + bi
                bu_ref[pl.ds(r, nb), off:off + cs] = nhr
                bu_ref[pl.ds(r, nb), off + cs:off + 2 * cs] = nhi
                new.append((nhr, nhi))
            return tuple(new)

        init = tuple((h_ref[:, off:off + cs], h_ref[:, off + cs:off + 2 * cs]) for off in offs)
        fin = lax.fori_loop(0, steps, step, init)
        for off, (hr, hi) in zip(offs, fin):
            h_ref[:, off:off + cs] = hr
            h_ref[:, off + cs:off + 2 * cs] = hi

    y = jnp.concatenate(
        [_dot(bu_ref[:, 2 * cs * c:2 * cs * (c + 1)].astype(BF16), wc_ref[c]) for c in range(S5_CHUNKS)],
        axis=1)
    y = _gelu_tanh(y + d_ref[...] * u)
    gl = _dot(y.astype(BF16), wglu_ref[...])
    out = gl[:, :S5_WIDTH] * jax.nn.sigmoid(gl[:, S5_WIDTH:])
    out_ref[...] = pltpu.einshape("tbc->btc", out.reshape(steps, nb, S5_WIDTH))


def _s5(u, lr, li, wb, wc, d, wglu, *, steps):
    batch, s, _ = u.shape
    rows = steps * batch
    const = lambda a: pl.BlockSpec(a.shape, lambda i: (0,) * a.ndim)
    blk = pl.BlockSpec((batch, steps, S5_WIDTH), lambda i: (0, i, 0))
    return pl.pallas_call(
        functools.partial(_s5_kernel, steps=steps),
        grid=(s // steps,),
        in_specs=[blk, const(lr), const(li), const(wb), const(wc), const(d), const(wglu)],
        out_specs=blk,
        out_shape=jax.ShapeDtypeStruct((batch, s, S5_WIDTH), F32),
        scratch_shapes=[pltpu.VMEM((rows, 2 * S5_CHUNKS * S5_CHUNK_STATE), F32),
                        pltpu.VMEM((batch, 2 * S5_CHUNKS * S5_CHUNK_STATE), F32)],
        compiler_params=_cparams(1), name="s5_mixer",
    )(u, lr, li, wb, wc, d, wglu)


def _merge_kernel(x_ref, g_ref, up_ref, o0_ref, l0_ref, o1_ref, l1_ref, o2_ref, l2_ref, ysb_ref, ys5_ref,
                  pw_ref, ps_ref, wg_ref, wb_ref, wo_ref, out_ref, ext_ref, no1_ref, nl1_ref, no2_ref, nl2_ref,
                  *, tm):
    ti = pl.program_id(1)
    x = x_ref[...]
    xn = _rms(x, g_ref[...]).astype(BF16)

    @pl.when(ti == 0)
    def _():
        ext_ref[:POOL_HALO, :] = jnp.zeros((POOL_HALO, POOL_WIDTH), F32)

    ext_ref[POOL_HALO:, :] = up_ref[...]
    pos = ti * tm + lax.broadcasted_iota(jnp.int32, (tm, 1), 0)
    ys = []
    for gi, w in enumerate(POOL_WINDOWS):
        c0, c1 = gi * POOL_GROUP, (gi + 1) * POOL_GROUP
        acc = ext_ref[pl.ds(POOL_HALO, tm), c0:c1]
        for j in range(1, w):
            acc = acc + ext_ref[pl.ds(POOL_HALO - j, tm), c0:c1]
        cnt = jnp.minimum(pos + 1, w).astype(F32)
        p = acc / cnt - ext_ref[pl.ds(POOL_HALO, tm), c0:c1]
        ys.append(_dot(p.astype(BF16), pw_ref[gi]))
    y_pool = jnp.concatenate(ys, axis=1) * ps_ref[...]
    ext_ref[:POOL_HALO, :] = ext_ref[pl.ds(tm, POOL_HALO), :]

    nslab = DIL_GROUP_WIDTH // LANES

    def to_token_order(src_ref, dst_ref):
        dil = src_ref.shape[0]
        for r in range(dil):
            for sl in range(nslab):
                dst_ref[sl, pl.ds(r, tm // dil, stride=dil), :] = src_ref[r, :, sl * LANES:(sl + 1) * LANES]
        return jnp.concatenate([dst_ref[sl] for sl in range(nslab)], axis=1)

    o1, l1 = to_token_order(o1_ref, no1_ref), to_token_order(l1_ref, nl1_ref)
    o2, l2 = to_token_order(o2_ref, no2_ref), to_token_order(l2_ref, nl2_ref)
    l0 = l0_ref[...]
    m = jnp.maximum(jnp.maximum(l0, l1), l2)
    e0, e1, e2 = jnp.exp(l0 - m), jnp.exp(l1 - m), jnp.exp(l2 - m)
    y_dil = (e0 * o0_ref[...] + e1 * o1 + e2 * o2) / (e0 + e1 + e2)

    branches = (y_pool.astype(BF16), y_dil.astype(BF16), ysb_ref[...], ys5_ref[...].astype(BF16))
    merged = jnp.zeros((tm, x.shape[1]), F32)
    for bi, yb in enumerate(branches):
        gate = jax.nn.sigmoid(_dot(xn, wg_ref[bi]))
        merged = merged + gate * _dot(yb, wb_ref[BRANCH_ROWS[bi]:BRANCH_ROWS[bi + 1], :])
    out_ref[...] = x + _dot(merged.astype(BF16), wo_ref[...])


def _merge(x, g, up, o0, l0, o1, l1, o2, l2, ysb, ys5, pw, ps, wg, wb, wo, *, tm):
    b, s, d = x.shape
    dil1, dil2 = o1.shape[1], o2.shape[1]
    gw = DIL_GROUP_WIDTH
    tok = lambda w: pl.BlockSpec((None, tm, w), lambda bi, ti: (bi, ti, 0))
    res = lambda dil: pl.BlockSpec((None, dil, tm // dil, gw), lambda bi, ti: (bi, 0, ti, 0))
    const = lambda a: pl.BlockSpec(a.shape, lambda bi, ti: (0,) * a.ndim)
    in_specs = [tok(d), const(g), tok(POOL_WIDTH), tok(gw), tok(gw), res(dil1), res(dil1), res(dil2), res(dil2),
                tok(SB_WIDTH), tok(S5_WIDTH),
                const(pw), const(ps), const(wg), const(wb), const(wo)]
    return pl.pallas_call(
        functools.partial(_merge_kernel, tm=tm),
        grid=(b, s // tm), in_specs=in_specs, out_specs=tok(d),
        out_shape=jax.ShapeDtypeStruct((b, s, d), F32),
        scratch_shapes=[pltpu.VMEM((tm + POOL_HALO, POOL_WIDTH), F32)]
        + [pltpu.VMEM((gw // LANES, tm, LANES), F32)] * 4,
        compiler_params=_cparams(2), name="gated_merge",
    )(x, g, up, o0, l0, o1, l1, o2, l2, ysb, ys5, pw, ps, wg, wb, wo)


def _ffn_kernel(x_ref, g_ref, wup_ref, wdn_ref, fg_ref, out_ref, *, hidden, final):
    x = x_ref[...]
    hn = _rms(x, g_ref[...]).astype(BF16)
    gu = _dot(hn, wup_ref[...])
    h = (jax.nn.silu(gu[:, :hidden]) * gu[:, hidden:]).astype(BF16)
    y = x + _dot(h, wdn_ref[...])
    if final:
        y = _rms(y, fg_ref[...])
    out_ref[...] = y


def _ffn(x2d, g, wup, wdn, fg, *, tm, final):
    n, d = x2d.shape
    hidden = wdn.shape[0]
    const = lambda a: pl.BlockSpec(a.shape, lambda i: (0,) * a.ndim, pipeline_mode=pl.Buffered(1))
    return pl.pallas_call(
        functools.partial(_ffn_kernel, hidden=hidden, final=final),
        grid=(n // tm,),
        in_specs=[pl.BlockSpec((tm, d), lambda i: (i, 0)), const(g), const(wup), const(wdn), const(fg)],
        out_specs=pl.BlockSpec((tm, d), lambda i: (i, 0)),
        out_shape=jax.ShapeDtypeStruct((n, d), F32),
        compiler_params=_cparams(1), name="swiglu_final" if final else "swiglu",
    )(x2d, g, wup, wdn, fg)


def _t5_bucket(dist):
    exact = REL_BUCKETS // 2
    df = jnp.maximum(dist, 1).astype(F32)
    large = exact + (jnp.log(df / exact) / math.log(REL_MAX_DIST / exact)
                     * (REL_BUCKETS - exact)).astype(jnp.int32)
    large = jnp.minimum(large, REL_BUCKETS - 1)
    return jnp.where(dist < exact, dist, large)


def _dil_bias(rel_bias_g, dil):
    band = DIL_BAND
    i = jnp.arange(band)[:, None]
    c = jnp.arange(2 * band)[None, :]
    dist_sub = band + i - c
    in_band = (dist_sub >= 0) & (dist_sub <= band)
    buckets = _t5_bucket(jnp.clip(dist_sub, 0, band) * dil)
    onehot = (buckets[None] == jnp.arange(REL_BUCKETS)[:, None, None]).astype(F32)
    bias = jnp.einsum('kh,kic->hic', rel_bias_g.astype(F32), onehot, precision=lax.Precision.HIGHEST)
    bias = jnp.where(in_band[None], bias, NEG_INF)
    first = jnp.where((c >= band)[None], bias, NEG_INF)
    return jnp.stack([bias, first])


def _dil_weights(w_dil, group):
    hw = DIL_HEADS * DIL_HEAD_DIM
    cols = [w_dil[:, i * hw + group * DIL_GROUP_WIDTH:i * hw + (group + 1) * DIL_GROUP_WIDTH] for i in range(3)]
    return jnp.concatenate(cols, axis=1).astype(BF16)


def _s5_params(a_re, a_im, log_dt, b_re, b_im, c_re, c_im):
    lam = lax.complex(a_re.astype(F32), a_im.astype(F32))
    dt = jnp.exp(log_dt.astype(F32))[:, None]
    lam_bar = jnp.exp(lam * dt)
    b_bar = ((lam_bar - 1.0) / lam)[:, :, None] * lax.complex(b_re.astype(F32), b_im.astype(F32))
    gpc = S5_GROUPS // S5_CHUNKS
    eye = jnp.eye(gpc, dtype=F32)

    def pack_b(bm):
        bm = bm.reshape(S5_CHUNKS, gpc, S5_STATE, S5_CH)
        return jnp.einsum('cgpk,gh->cgkhp', bm, eye).reshape(S5_CHUNKS, S5_CHUNK_CH, S5_CHUNK_STATE)

    def pack_c(cm):
        cm = cm.reshape(S5_CHUNKS, gpc, S5_CH, S5_STATE)
        return jnp.einsum('cgkp,gh->cgphk', cm, eye).reshape(S5_CHUNKS, S5_CHUNK_STATE, S5_CHUNK_CH)

    wb = jnp.concatenate([pack_b(jnp.real(b_bar)), pack_b(jnp.imag(b_bar))], axis=2).astype(BF16)
    wc = jnp.concatenate([pack_c(c_re.astype(F32)), pack_c(-c_im.astype(F32))], axis=1).astype(BF16)
    lr = jnp.real(lam_bar).reshape(S5_CHUNKS, 1, S5_CHUNK_STATE)
    li = jnp.imag(lam_bar).reshape(S5_CHUNKS, 1, S5_CHUNK_STATE)
    return lr, li, wb, wc


def kernel(x, attn_norm_g, w_in, pool_w, pool_scale, rel_bias, s5_a_re, s5_a_im, s5_log_dt, s5_b_re, s5_b_im,
           s5_c_re, s5_c_im, s5_d, s5_w_glu, w_branch, w_gate, w_out, ffn_norm_g, w_up, w_down, final_norm_g):
    b, s, d = x.shape
    depth = w_in.shape[0]
    tm = 512
    o1 = POOL_WIDTH
    o2 = o1 + 3 * DIL_HEADS * DIL_HEAD_DIM
    o3 = o2 + 3 * SB_WIDTH

    t = SB_TILE
    ntri = -(jnp.arange(t)[None, :] >= jnp.arange(t)[:, None]).astype(BF16)
    biases = [_dil_bias(rel_bias[:, g * DIL_HEADS_PER_GROUP:(g + 1) * DIL_HEADS_PER_GROUP], dil)
              for g, (_, dil) in enumerate(DIL_PAIRS)]
    key_head = jnp.arange(DIL_HEADS_PER_GROUP * 2 * DIL_BAND) // (2 * DIL_BAND)
    ones_bd = (key_head[:, None] == (jnp.arange(DIL_GROUP_WIDTH) // DIL_HEAD_DIM)[None, :]).astype(BF16)
    fg = final_norm_g.reshape(1, d).astype(F32)

    for l in range(depth):
        wl = w_in[l]
        g_attn = attn_norm_g[l].reshape(1, d).astype(F32)
        w_dil = wl[:, o1:o2]
        w_sb = wl[:, o2:o3]
        up, d0, d1, d2, q, k, vt, u_s5 = _in_proj(
            x, g_attn, wl[:, :o1].astype(BF16),
            _dil_weights(w_dil, 0), _dil_weights(w_dil, 1), _dil_weights(w_dil, 2),
            w_sb[:, :SB_WIDTH].astype(BF16), w_sb[:, SB_WIDTH:2 * SB_WIDTH].astype(BF16),
            w_sb[:, 2 * SB_WIDTH:].T.astype(BF16), wl[:, o3:].astype(BF16), tm=tm)

        oa0, la0 = _dil_attn(d0.reshape(b, 1, s, d0.shape[-1]), biases[0], ones_bd)
        oa1, la1 = _dil_attn(d1, biases[1], ones_bd)
        oa2, la2 = _dil_attn(d2, biases[2], ones_bd)
        y_sb = _sb_attn(q, k, vt, ntri, tq=256)

        lr, li, wb, wc = _s5_params(s5_a_re[l], s5_a_im[l], s5_log_dt[l], s5_b_re[l], s5_b_im[l],
                                    s5_c_re[l], s5_c_im[l])
        y_s5 = _s5(u_s5, lr, li, wb, wc, s5_d[l].reshape(1, S5_WIDTH).astype(F32), s5_w_glu[l].astype(BF16),
                   steps=128)

        x = _merge(x, g_attn, up, oa0.reshape(b, s, -1), la0.reshape(b, s, -1), oa1, la1, oa2, la2,
                   y_sb, y_s5,
                   pool_w[l].astype(BF16), pool_scale[l].reshape(1, POOL_WIDTH).astype(F32),
                   w_gate[l].astype(BF16), w_branch[l].astype(BF16), w_out[l].astype(BF16), tm=256)

        x = _ffn(x.reshape(b * s, d), ffn_norm_g[l].reshape(1, d).astype(F32), w_up[l].astype(BF16),
                 w_down[l].astype(BF16), fg, tm=512, final=(l == depth - 1)).reshape(b, s, d)
    return x
```

```python
import functools
import math

import jax
import jax.numpy as jnp
from jax import lax
from jax.experimental import pallas as pl
from jax.experimental.pallas import tpu as pltpu

F32 = jnp.float32
BF16 = jnp.bfloat16

EPS = 1e-6
POOL_WINDOWS = (2, 4, 8, 16)
POOL_GROUP = 128
POOL_WIDTH = 512
POOL_HALO = 16
DIL_PAIRS = ((128, 1), (512, 4), (2048, 16))
DIL_BAND = 128
DIL_HEADS_PER_GROUP = 4
DIL_HEADS = 12
DIL_HEAD_DIM = 64
DIL_GROUP_WIDTH = DIL_HEADS_PER_GROUP * DIL_HEAD_DIM
DIL_STEP_ROWS = 1024
DIL_BLOCKS_PER_GROUP = 4
REL_BUCKETS = 32
REL_MAX_DIST = 2048
SB_HEADS = 4
SB_HEAD_DIM = 128
SB_WIDTH = 512
SB_TILE = 256
S5_WIDTH = 512
S5_CH = 16
S5_GROUPS = 32
S5_STATE = 64
S5_CHUNKS = 4
S5_CHUNK_CH = S5_WIDTH // S5_CHUNKS
S5_CHUNK_STATE = (S5_GROUPS // S5_CHUNKS) * S5_STATE
BRANCH_ROWS = (0, 512, 768, 1280, 1792)
NEG_INF = -1e30
SB_UNDERFLOW = 160.0
LANES = 128
LOG2E = math.log2(math.e)

VMEM_LIMIT = 56 * 1024 * 1024


def _cparams(n_axes):
    return pltpu.CompilerParams(dimension_semantics=("arbitrary",) * n_axes,
                                vmem_limit_bytes=VMEM_LIMIT)


def _dot(a, b):
    return jnp.dot(a, b, preferred_element_type=F32)


def _dot_nt(a, b):
    return lax.dot_general(a, b, (((1,), (1,)), ((), ())), preferred_element_type=F32)


def _rms(x, g):
    ms = jnp.mean(x * x, axis=-1, keepdims=True)
    return x * lax.rsqrt(ms + EPS) * g


def _in_proj_kernel(x_ref, g_ref, wpool_ref, wd0_ref, wd1_ref, wd2_ref, wq_ref, wk_ref, wvt_ref,
                    ws5_ref, pool_ref, d0_ref, d1_ref, d2_ref, q_ref, k_ref, vt_ref, s5_ref,
                    scr_ref, *, tm, sb_scale):
    xn = _rms(x_ref[...], g_ref[...]).astype(BF16)
    pool_ref[...] = _dot(xn, wpool_ref[...])
    s5_ref[...] = _dot(xn, ws5_ref[...])

    def dil_proj(w_ref):
        q = _dot(xn, w_ref[:, :DIL_GROUP_WIDTH]) * (1.0 / math.sqrt(DIL_HEAD_DIM))
        kv = _dot(xn, w_ref[:, DIL_GROUP_WIDTH:])
        return q, kv

    q0, kv0 = dil_proj(wd0_ref)
    d0_ref[:, :DIL_GROUP_WIDTH] = q0.astype(BF16)
    d0_ref[:, DIL_GROUP_WIDTH:] = kv0.astype(BF16)
    for w_ref, out_ref, dil in ((wd1_ref, d1_ref, DIL_PAIRS[1][1]), (wd2_ref, d2_ref, DIL_PAIRS[2][1])):
        qg, kvg = dil_proj(w_ref)
        qkv = jnp.concatenate([qg, kvg], axis=1)
        for sl in range(qkv.shape[1] // LANES):
            scr_ref[sl] = qkv[:, sl * LANES:(sl + 1) * LANES]
        for r in range(dil):
            for sl in range(qkv.shape[1] // LANES):
                out_ref[r, :, sl * LANES:(sl + 1) * LANES] = (
                    scr_ref[sl, pl.ds(r, tm // dil, stride=dil), :].astype(BF16))

    q_ref[...] = (_dot(xn, wq_ref[...]) * sb_scale).astype(BF16)
    k_ref[...] = _dot(xn, wk_ref[...]).astype(BF16)
    vt = _dot_nt(wvt_ref[...], xn).astype(BF16)
    for jb in range(tm // SB_TILE):
        vt_ref[jb] = vt[:, jb * SB_TILE:(jb + 1) * SB_TILE]


def _in_proj(x, g, wpool, wd0, wd1, wd2, wq, wk, wvt, ws5, *, tm):
    b, s, d = x.shape
    dil1, dil2 = DIL_PAIRS[1][1], DIL_PAIRS[2][1]
    gw = 3 * DIL_GROUP_WIDTH
    const = lambda shape: pl.BlockSpec(shape, lambda bi, ti: (0,) * len(shape))
    out_shape = (
        jax.ShapeDtypeStruct((b, s, POOL_WIDTH), F32),
        jax.ShapeDtypeStruct((b, s, gw), BF16),
        jax.ShapeDtypeStruct((b, dil1, s // dil1, gw), BF16),
        jax.ShapeDtypeStruct((b, dil2, s // dil2, gw), BF16),
        jax.ShapeDtypeStruct((b, s, SB_WIDTH), BF16),
        jax.ShapeDtypeStruct((b, s, SB_WIDTH), BF16),
        jax.ShapeDtypeStruct((b, s // SB_TILE, SB_WIDTH, SB_TILE), BF16),
        jax.ShapeDtypeStruct((b, s, S5_WIDTH), F32),
    )
    out_specs = (
        pl.BlockSpec((None, tm, POOL_WIDTH), lambda bi, ti: (bi, ti, 0)),
        pl.BlockSpec((None, tm, gw), lambda bi, ti: (bi, ti, 0)),
        pl.BlockSpec((None, dil1, tm // dil1, gw), lambda bi, ti: (bi, 0, ti, 0)),
        pl.BlockSpec((None, dil2, tm // dil2, gw), lambda bi, ti: (bi, 0, ti, 0)),
        pl.BlockSpec((None, tm, SB_WIDTH), lambda bi, ti: (bi, ti, 0)),
        pl.BlockSpec((None, tm, SB_WIDTH), lambda bi, ti: (bi, ti, 0)),
        pl.BlockSpec((None, tm // SB_TILE, SB_WIDTH, SB_TILE), lambda bi, ti: (bi, ti, 0, 0)),
        pl.BlockSpec((None, tm, S5_WIDTH), lambda bi, ti: (bi, ti, 0)),
    )
    in_specs = [
        pl.BlockSpec((None, tm, d), lambda bi, ti: (bi, ti, 0)),
        const(g.shape), const(wpool.shape), const(wd0.shape), const(wd1.shape), const(wd2.shape),
        const(wq.shape), const(wk.shape), const(wvt.shape), const(ws5.shape),
    ]
    sb_scale = LOG2E / math.sqrt(SB_HEAD_DIM)
    return pl.pallas_call(
        functools.partial(_in_proj_kernel, tm=tm, sb_scale=sb_scale),
        grid=(b, s // tm), in_specs=in_specs, out_specs=out_specs, out_shape=out_shape,
        scratch_shapes=[pltpu.VMEM((gw // LANES, tm, LANES), F32)],
        compiler_params=_cparams(2), name="in_proj",
    )(x, g, wpool, wd0, wd1, wd2, wq, wk, wvt, ws5)


def _dil_attn_kernel(cur_ref, halo_ref, bias_ref, ones_ref, o_ref, lse_ref):
    nres, rows, _ = cur_ref.shape
    nblk = rows // DIL_BAND
    gw = DIL_GROUP_WIDTH
    nh = DIL_HEADS_PER_GROUP
    step = pl.program_id(2)
    lane_head = lax.broadcasted_iota(jnp.int32, (1, gw), 1) // DIL_HEAD_DIM
    hms = [lane_head == h for h in range(nh)]
    hmuls = [hm.astype(BF16) for hm in hms]
    ones_bd = ones_ref[...]

    def blocks(items):
        r0s = [n * DIL_BAND if isinstance(n, int) else pl.multiple_of(n * DIL_BAND, DIL_BAND) for _, n in items]
        kvs = []
        for (res, n), r0 in zip(items, r0s):
            if isinstance(n, int) and n == 0:
                kvs.append(jnp.concatenate([halo_ref[res, :, gw:], cur_ref[res, :DIL_BAND, gw:]], axis=0))
            else:
                kvs.append(cur_ref[res, pl.ds(r0 - DIL_BAND, 2 * DIL_BAND), gw:])
        ss = []
        for (res, _), r0, kvb in zip(items, r0s, kvs):
            q = cur_ref[res, pl.ds(r0, DIL_BAND), :gw]
            qs = jnp.concatenate([q * hmul for hmul in hmuls], axis=0)
            ss.append(_dot_nt(qs, kvb[:, :gw]))
        ps, ms = [], []
        for (_, n), s in zip(items, ss):
            if isinstance(n, int) and n > 0:
                first = 0
            else:
                first = (step * nblk + n == 0).astype(jnp.int32)
            p_h, m_h = [], []
            for h in range(nh):
                sh = s[h * DIL_BAND:(h + 1) * DIL_BAND, :] + bias_ref[first, h]
                m = jnp.max(sh, axis=-1, keepdims=True)
                p_h.append(jnp.exp(sh - m).astype(BF16))
                m_h.append(m)
            ps.append(jnp.concatenate(p_h, axis=1))
            ms.append(m_h)
        accs, dens = [], []
        for p, kvb in zip(ps, kvs):
            v = kvb[:, gw:]
            v_bd = jnp.concatenate([v * hmul for hmul in hmuls], axis=0)
            accs.append(_dot(p, v_bd))
            dens.append(_dot(p, ones_bd))
        for (res, _), r0, acc, den, m_h in zip(items, r0s, accs, dens, ms):
            m_b = m_h[0]
            for h in range(1, nh):
                m_b = jnp.where(hms[h], m_h[h], m_b)
            o_ref[res, pl.ds(r0, DIL_BAND), :] = acc / den
            lse_ref[res, pl.ds(r0, DIL_BAND), :] = m_b + jnp.log(den)

    group = DIL_BLOCKS_PER_GROUP
    if nres > 1:
        items = [(res, n) for res in range(nres) for n in range(nblk)]
        for g0 in range(0, len(items), group):
            blocks(items[g0:g0 + group])
    else:
        blocks([(0, n) for n in range(group)])

        def body(g, carry):
            blocks([(0, g * group + u) for u in range(group)])
            return carry

        lax.fori_loop(1, nblk // group, body, 0)


def _dil_attn(qkv, bias, ones_bd):
    b, dil, length, gw3 = qkv.shape
    rows = min(length, DIL_STEP_ROWS)
    nres = DIL_STEP_ROWS // rows
    rb = rows // DIL_BAND
    out_shape = (jax.ShapeDtypeStruct((b, dil, length, DIL_GROUP_WIDTH), F32),) * 2
    blk = lambda w: pl.BlockSpec((None, nres, rows, w), lambda bi, ri, i: (bi, ri, i, 0))
    halo = pl.BlockSpec((None, nres, DIL_BAND, gw3),
                        lambda bi, ri, i: (bi, ri, jnp.maximum(i * rb - 1, 0), 0))
    return pl.pallas_call(
        _dil_attn_kernel,
        grid=(b, dil // nres, length // rows),
        in_specs=[blk(gw3), halo, pl.BlockSpec(bias.shape, lambda bi, ri, i: (0, 0, 0, 0)),
                  pl.BlockSpec(ones_bd.shape, lambda bi, ri, i: (0, 0))],
        out_specs=(blk(DIL_GROUP_WIDTH), blk(DIL_GROUP_WIDTH)), out_shape=out_shape,
        compiler_params=_cparams(3), name=f"dil_attn_d{dil}",
    )(qkv, qkv, bias, ones_bd)


def _sb_attn_kernel(q_ref, k_ref, vt_ref, ntri_ref, o_ref, acc_ref):
    tk, tq = SB_TILE, q_ref.shape[0]
    i = pl.program_id(1)
    ntri = ntri_ref[...]
    row = lax.broadcasted_iota(jnp.int32, (tk, tq), 0)
    col = lax.broadcasted_iota(jnp.int32, (tk, tq), 1)
    sign = jnp.uint32(0x80000000)
    heads = range(SB_HEADS)
    hsl = [slice(h * SB_HEAD_DIM, (h + 1) * SB_HEAD_DIM) for h in heads]

    def tiles(specs, runs, first):
        chains = [(j, mask, h) for j, mask in specs for h in heads]
        kq = lambda j, h: _dot_nt(k_ref[pl.ds(pl.multiple_of(j * tk, tk), tk), hsl[h]], q_ref[:, hsl[h]])
        zs = [kq(j, h) for j, _, h in chains]
        his, z0s = [], []
        for (_, mask, _), z in zip(chains, zs):
            neg_abs = lax.bitcast_convert_type(lax.bitcast_convert_type(z, jnp.uint32) | sign, F32)
            sp = jnp.maximum(z, 0.0) + jnp.log(1.0 + jnp.exp2(neg_abs)) * LOG2E
            if mask is not None:
                sp = jnp.where(mask, sp, 0.0)
            his.append(sp.astype(BF16))
            z0s.append(z[0:1, :])
        ds = [kq(j, h) + _dot(ntri, hi) for (j, _, h), hi in zip(chains, his)]
        runs = list(runs)
        ws = []
        for (_, mask, h), z0, d in zip(chains, z0s, ds):
            w = jnp.exp2(d - runs[h])
            if mask is not None:
                w = jnp.where(mask, w, 0.0)
            ws.append(w.astype(BF16))
            runs[h] = runs[h] + (z0 - d[0:1, :])
        pvs = [_dot(vt_ref[j, hsl[h], :], w) for (j, _, h), w in zip(chains, ws)]
        for h in heads:
            total = pvs[h]
            for t in range(1, len(specs)):
                total = total + pvs[t * SB_HEADS + h]
            if first:
                acc_ref[h] = total
            else:
                acc_ref[h] += total
        return tuple(runs)

    zero_runs = (jnp.zeros((1, tq), F32),) * SB_HEADS
    causal = row < col

    def min_run(runs):
        return jnp.min(jnp.minimum(jnp.minimum(runs[0], runs[1]), jnp.minimum(runs[2], runs[3])))

    @pl.when(i == 0)
    def _():
        tiles([(0, causal)], zero_runs, True)

    @pl.when(i > 0)
    def _():
        runs = tiles([(i, causal), (i - 1, None)], zero_runs, True)
        nfull = i - 1

        def cond(carry):
            jj, mr, _ = carry
            return jnp.logical_and(jj < nfull, mr < SB_UNDERFLOW)

        def body(carry):
            jj, _, runs = carry
            runs = tiles([(nfull - 1 - jj, None)], runs, False)
            return jj + 1, min_run(runs), runs

        lax.while_loop(cond, body, (jnp.int32(0), min_run(runs), runs))

    for h in heads:
        o_ref[:, hsl[h]] = acc_ref[h].T.astype(o_ref.dtype)


def _sb_attn(q, k, vt, ntri, *, tq):
    b, s, _ = q.shape
    tk = SB_TILE
    return pl.pallas_call(
        _sb_attn_kernel,
        grid=(b, s // tq),
        in_specs=[
            pl.BlockSpec((None, tq, SB_WIDTH), lambda bi, i: (bi, i, 0)),
            pl.BlockSpec((None, s, SB_WIDTH), lambda bi, i: (bi, 0, 0)),
            pl.BlockSpec((None, s // tk, SB_WIDTH, tk), lambda bi, i: (bi, 0, 0, 0)),
            pl.BlockSpec((tk, tk), lambda bi, i: (0, 0)),
        ],
        out_specs=pl.BlockSpec((None, tq, SB_WIDTH), lambda bi, i: (bi, i, 0)),
        out_shape=jax.ShapeDtypeStruct((b, s, SB_WIDTH), BF16),
        scratch_shapes=[pltpu.VMEM((SB_HEADS, SB_HEAD_DIM, tq), F32)],
        compiler_params=_cparams(2), name="sb_attn",
    )(q, k, vt, ntri)


def _gelu_tanh(x):
    return 0.5 * x * (1.0 + jnp.tanh(math.sqrt(2.0 / math.pi) * (x + 0.044715 * (x * x * x))))


def _s5_kernel(u_ref, lr_ref, li_ref, wb_ref, wc_ref, d_ref, wglu_ref, out_ref, bu_ref, h_ref, *, steps):
    cs = S5_CHUNK_STATE

    @pl.when(pl.program_id(0) == 0)
    def _():
        h_ref[...] = jnp.zeros_like(h_ref)

    nb = h_ref.shape[0]
    u = pltpu.einshape("btc->tbc", u_ref[...]).reshape(steps * nb, S5_WIDTH)
    ub = u.astype(BF16)
    for c in range(S5_CHUNKS):
        bu_ref[:, 2 * cs * c:2 * cs * (c + 1)] = _dot(ub[:, S5_CHUNK_CH * c:S5_CHUNK_CH * (c + 1)], wb_ref[c])

    for c0 in range(0, S5_CHUNKS, 2):
        lam = [(jnp.broadcast_to(lr_ref[c], (nb, cs)), jnp.broadcast_to(li_ref[c], (nb, cs)))
               for c in (c0, c0 + 1)]
        offs = [2 * cs * c for c in (c0, c0 + 1)]

        def step(t, carry):
            r = pl.multiple_of(t * nb, nb)
            new = []
            for (lr, li), off, (hr, hi) in zip(lam, offs, carry):
                br = bu_ref[pl.ds(r, nb), off:off + cs]
                bi = bu_ref[pl.ds(r, nb), off + cs:off + 2 * cs]
                nhr = lr * hr - li * hi + br
                nhi = lr * hi + li * hr + bi
                bu_ref[pl.ds(r, nb), off:off + cs] = nhr
                bu_ref[pl.ds(r, nb), off + cs:off + 2 * cs] = nhi
                new.append((nhr, nhi))
            return tuple(new)

        init = tuple((h_ref[:, off:off + cs], h_ref[:, off + cs:off + 2 * cs]) for off in offs)
        fin = lax.fori_loop(0, steps, step, init)
        for off, (hr, hi) in zip(offs, fin):
            h_ref[:, off:off + cs] = hr
            h_ref[:, off + cs:off + 2 * cs] = hi

    y = jnp.concatenate(
        [_dot(bu_ref[:, 2 * cs * c:2 * cs * (c + 1)].astype(BF16), wc_ref[c]) for c in range(S5_CHUNKS)],
        axis=1)
    y = _gelu_tanh(y + d_ref[...] * u)
    gl = _dot(y.astype(BF16), wglu_ref[...])
    out = gl[:, :S5_WIDTH] * jax.nn.sigmoid(gl[:, S5_WIDTH:])
    out_ref[...] = pltpu.einshape("tbc->btc", out.reshape(steps, nb, S5_WIDTH))


def _s5(u, lr, li, wb, wc, d, wglu, *, steps):
    batch, s, _ = u.shape
    rows = steps * batch
    const = lambda a: pl.BlockSpec(a.shape, lambda i: (0,) * a.ndim)
    blk = pl.BlockSpec((batch, steps, S5_WIDTH), lambda i: (0, i, 0))
    return pl.pallas_call(
        functools.partial(_s5_kernel, steps=steps),
        grid=(s // steps,),
        in_specs=[blk, const(lr), const(li), const(wb), const(wc), const(d), const(wglu)],
        out_specs=blk,
        out_shape=jax.ShapeDtypeStruct((batch, s, S5_WIDTH), F32),
        scratch_shapes=[pltpu.VMEM((rows, 2 * S5_CHUNKS * S5_CHUNK_STATE), F32),
                        pltpu.VMEM((batch, 2 * S5_CHUNKS * S5_CHUNK_STATE), F32)],
        compiler_params=_cparams(1), name="s5_mixer",
    )(u, lr, li, wb, wc, d, wglu)


def _merge_kernel(x_ref, g_ref, up_ref, o0_ref, l0_ref, o1_ref, l1_ref, o2_ref, l2_ref, ysb_ref, ys5_ref,
                  pw_ref, ps_ref, wg_ref, wb_ref, wo_ref, out_ref, ext_ref, no1_ref, nl1_ref, no2_ref, nl2_ref,
                  *, tm):
    ti = pl.program_id(1)
    x = x_ref[...]
    xn = _rms(x, g_ref[...]).astype(BF16)

    @pl.when(ti == 0)
    def _():
        ext_ref[:POOL_HALO, :] = jnp.zeros((POOL_HALO, POOL_WIDTH), F32)

    ext_ref[POOL_HALO:, :] = up_ref[...]
    pos = ti * tm + lax.broadcasted_iota(jnp.int32, (tm, 1), 0)
    ys = []
    for gi, w in enumerate(POOL_WINDOWS):
        c0, c1 = gi * POOL_GROUP, (gi + 1) * POOL_GROUP
        acc = ext_ref[pl.ds(POOL_HALO, tm), c0:c1]
        for j in range(1, w):
            acc = acc + ext_ref[pl.ds(POOL_HALO - j, tm), c0:c1]
        cnt = jnp.minimum(pos + 1, w).astype(F32)
        p = acc / cnt - ext_ref[pl.ds(POOL_HALO, tm), c0:c1]
        ys.append(_dot(p.astype(BF16), pw_ref[gi]))
    y_pool = jnp.concatenate(ys, axis=1) * ps_ref[...]
    ext_ref[:POOL_HALO, :] = ext_ref[pl.ds(tm, POOL_HALO), :]

    nslab = DIL_GROUP_WIDTH // LANES

    def to_token_order(src_ref, dst_ref):
        dil = src_ref.shape[0]
        for r in range(dil):
            for sl in range(nslab):
                dst_ref[sl, pl.ds(r, tm // dil, stride=dil), :] = src_ref[r, :, sl * LANES:(sl + 1) * LANES]
        return jnp.concatenate([dst_ref[sl] for sl in range(nslab)], axis=1)

    o1, l1 = to_token_order(o1_ref, no1_ref), to_token_order(l1_ref, nl1_ref)
    o2, l2 = to_token_order(o2_ref, no2_ref), to_token_order(l2_ref, nl2_ref)
    l0 = l0_ref[...]
    m = jnp.maximum(jnp.maximum(l0, l1), l2)
    e0, e1, e2 = jnp.exp(l0 - m), jnp.exp(l1 - m), jnp.exp(l2 - m)
    y_dil = (e0 * o0_ref[...] + e1 * o1 + e2 * o2) / (e0 + e1 + e2)

    branches = (y_pool.astype(BF16), y_dil.astype(BF16), ysb_ref[...], ys5_ref[...].astype(BF16))
    merged = jnp.zeros((tm, x.shape[1]), F32)
    for bi, yb in enumerate(branches):
        gate = jax.nn.sigmoid(_dot(xn, wg_ref[bi]))
        merged = merged + gate * _dot(yb, wb_ref[BRANCH_ROWS[bi]:BRANCH_ROWS[bi + 1], :])
    out_ref[...] = x + _dot(merged.astype(BF16), wo_ref[...])


def _merge(x, g, up, o0, l0, o1, l1, o2, l2, ysb, ys5, pw, ps, wg, wb, wo, *, tm):
    b, s, d = x.shape
    dil1, dil2 = o1.shape[1], o2.shape[1]
    gw = DIL_GROUP_WIDTH
    tok = lambda w: pl.BlockSpec((None, tm, w), lambda bi, ti: (bi, ti, 0))
    res = lambda dil: pl.BlockSpec((None, dil, tm // dil, gw), lambda bi, ti: (bi, 0, ti, 0))
    const = lambda a: pl.BlockSpec(a.shape, lambda bi, ti: (0,) * a.ndim)
    in_specs = [tok(d), const(g), tok(POOL_WIDTH), tok(gw), tok(gw), res(dil1), res(dil1), res(dil2), res(dil2),
                tok(SB_WIDTH), tok(S5_WIDTH),
                const(pw), const(ps), const(wg), const(wb), const(wo)]
    return pl.pallas_call(
        functools.partial(_merge_kernel, tm=tm),
        grid=(b, s // tm), in_specs=in_specs, out_specs=tok(d),
        out_shape=jax.ShapeDtypeStruct((b, s, d), F32),
        scratch_shapes=[pltpu.VMEM((tm + POOL_HALO, POOL_WIDTH), F32)]
        + [pltpu.VMEM((gw // LANES, tm, LANES), F32)] * 4,
        compiler_params=_cparams(2), name="gated_merge",
    )(x, g, up, o0, l0, o1, l1, o2, l2, ysb, ys5, pw, ps, wg, wb, wo)


def _ffn_kernel(x_ref, g_ref, wup_ref, wdn_ref, fg_ref, out_ref, *, hidden, final):
    x = x_ref[...]
    hn = _rms(x, g_ref[...]).astype(BF16)
    gu = _dot(hn, wup_ref[...])
    h = (jax.nn.silu(gu[:, :hidden]) * gu[:, hidden:]).astype(BF16)
    y = x + _dot(h, wdn_ref[...])
    if final:
        y = _rms(y, fg_ref[...])
    out_ref[...] = y


def _ffn(x2d, g, wup, wdn, fg, *, tm, final):
    n, d = x2d.shape
    hidden = wdn.shape[0]
    const = lambda a: pl.BlockSpec(a.shape, lambda i: (0,) * a.ndim, pipeline_mode=pl.Buffered(1))
    return pl.pallas_call(
        functools.partial(_ffn_kernel, hidden=hidden, final=final),
        grid=(n // tm,),
        in_specs=[pl.BlockSpec((tm, d), lambda i: (i, 0)), const(g), const(wup), const(wdn), const(fg)],
        out_specs=pl.BlockSpec((tm, d), lambda i: (i, 0)),
        out_shape=jax.ShapeDtypeStruct((n, d), F32),
        compiler_params=_cparams(1), name="swiglu_final" if final else "swiglu",
    )(x2d, g, wup, wdn, fg)


def _t5_bucket(dist):
    exact = REL_BUCKETS // 2
    df = jnp.maximum(dist, 1).astype(F32)
    large = exact + (jnp.log(df / exact) / math.log(REL_MAX_DIST / exact)
                     * (REL_BUCKETS - exact)).astype(jnp.int32)
    large = jnp.minimum(large, REL_BUCKETS - 1)
    return jnp.where(dist < exact, dist, large)


def _dil_bias(rel_bias_g, dil):
    band = DIL_BAND
    i = jnp.arange(band)[:, None]
    c = jnp.arange(2 * band)[None, :]
    dist_sub = band + i - c
    in_band = (dist_sub >= 0) & (dist_sub <= band)
    buckets = _t5_bucket(jnp.clip(dist_sub, 0, band) * dil)
    onehot = (buckets[None] == jnp.arange(REL_BUCKETS)[:, None, None]).astype(F32)
    bias = jnp.einsum('kh,kic->hic', rel_bias_g.astype(F32), onehot, precision=lax.Precision.HIGHEST)
    bias = jnp.where(in_band[None], bias, NEG_INF)
    first = jnp.where((c >= band)[None], bias, NEG_INF)
    return jnp.stack([bias, first])


def _dil_weights(w_dil, group):
    hw = DIL_HEADS * DIL_HEAD_DIM
    cols = [w_dil[:, i * hw + group * DIL_GROUP_WIDTH:i * hw + (group + 1) * DIL_GROUP_WIDTH] for i in range(3)]
    return jnp.concatenate(cols, axis=1).astype(BF16)


def _s5_params(a_re, a_im, log_dt, b_re, b_im, c_re, c_im):
    lam = lax.complex(a_re.astype(F32), a_im.astype(F32))
    dt = jnp.exp(log_dt.astype(F32))[:, None]
    lam_bar = jnp.exp(lam * dt)
    b_bar = ((lam_bar - 1.0) / lam)[:, :, None] * lax.complex(b_re.astype(F32), b_im.astype(F32))
    gpc = S5_GROUPS // S5_CHUNKS
    eye = jnp.eye(gpc, dtype=F32)

    def pack_b(bm):
        bm = bm.reshape(S5_CHUNKS, gpc, S5_STATE, S5_CH)
        return jnp.einsum('cgpk,gh->cgkhp', bm, eye).reshape(S5_CHUNKS, S5_CHUNK_CH, S5_CHUNK_STATE)

    def pack_c(cm):
        cm = cm.reshape(S5_CHUNKS, gpc, S5_CH, S5_STATE)
        return jnp.einsum('cgkp,gh->cgphk', cm, eye).reshape(S5_CHUNKS, S5_CHUNK_STATE, S5_CHUNK_CH)

    wb = jnp.concatenate([pack_b(jnp.real(b_bar)), pack_b(jnp.imag(b_bar))], axis=2).astype(BF16)
    wc = jnp.concatenate([pack_c(c_re.astype(F32)), pack_c(-c_im.astype(F32))], axis=1).astype(BF16)
    lr = jnp.real(lam_bar).reshape(S5_CHUNKS, 1, S5_CHUNK_STATE)
    li = jnp.imag(lam_bar).reshape(S5_CHUNKS, 1, S5_CHUNK_STATE)
    return lr, li, wb, wc


def kernel(x, attn_norm_g, w_in, pool_w, pool_scale, rel_bias, s5_a_re, s5_a_im, s5_log_dt, s5_b_re, s5_b_im,
           s5_c_re, s5_c_im, s5_d, s5_w_glu, w_branch, w_gate, w_out, ffn_norm_g, w_up, w_down, final_norm_g):
    b, s, d = x.shape
    depth = w_in.shape[0]
    tm = 512
    o1 = POOL_WIDTH
    o2 = o1 + 3 * DIL_HEADS * DIL_HEAD_DIM
    o3 = o2 + 3 * SB_WIDTH

    t = SB_TILE
    ntri = -(jnp.arange(t)[None, :] >= jnp.arange(t)[:, None]).astype(BF16)
    biases = [_dil_bias(rel_bias[:, g * DIL_HEADS_PER_GROUP:(g + 1) * DIL_HEADS_PER_GROUP], dil)
              for g, (_, dil) in enumerate(DIL_PAIRS)]
    key_head = jnp.arange(DIL_HEADS_PER_GROUP * 2 * DIL_BAND) // (2 * DIL_BAND)
    ones_bd = (key_head[:, None] == (jnp.arange(DIL_GROUP_WIDTH) // DIL_HEAD_DIM)[None, :]).astype(BF16)
    fg = final_norm_g.reshape(1, d).astype(F32)

    for l in range(depth):
        wl = w_in[l]
        g_attn = attn_norm_g[l].reshape(1, d).astype(F32)
        w_dil = wl[:, o1:o2]
        w_sb = wl[:, o2:o3]
        up, d0, d1, d2, q, k, vt, u_s5 = _in_proj(
            x, g_attn, wl[:, :o1].astype(BF16),
            _dil_weights(w_dil, 0), _dil_weights(w_dil, 1), _dil_weights(w_dil, 2),
            w_sb[:, :SB_WIDTH].astype(BF16), w_sb[:, SB_WIDTH:2 * SB_WIDTH].astype(BF16),
            w_sb[:, 2 * SB_WIDTH:].T.astype(BF16), wl[:, o3:].astype(BF16), tm=tm)

        oa0, la0 = _dil_attn(d0.reshape(b, 1, s, d0.shape[-1]), biases[0], ones_bd)
        oa1, la1 = _dil_attn(d1, biases[1], ones_bd)
        oa2, la2 = _dil_attn(d2, biases[2], ones_bd)
        y_sb = _sb_attn(q, k, vt, ntri, tq=256)

        lr, li, wb, wc = _s5_params(s5_a_re[l], s5_a_im[l], s5_log_dt[l], s5_b_re[l], s5_b_im[l],
                                    s5_c_re[l], s5_c_im[l])
        y_s5 = _s5(u_s5, lr, li, wb, wc, s5_d[l].reshape(1, S5_WIDTH).astype(F32), s5_w_glu[l].astype(BF16),
                   steps=128)

        x = _merge(x, g_attn, up, oa0.reshape(b, s, -1), la0.reshape(b, s, -1), oa1, la1, oa2, la2,
                   y_sb, y_s5,
                   pool_w[l].astype(BF16), pool_scale[l].reshape(1, POOL_WIDTH).astype(F32),
                   w_gate[l].astype(BF16), w_branch[l].astype(BF16), w_out[l].astype(BF16), tm=256)

        x = _ffn(x.reshape(b * s, d), ffn_norm_g[l].reshape(1, d).astype(F32), w_up[l].astype(BF16),
                 w_down[l].astype(BF16), fg, tm=512, final=(l == depth - 1)).reshape(b, s, d)
    return x
```

```python
import functools
import math

import jax
import jax.numpy as jnp
from jax import lax
from jax.experimental import pallas as pl
from jax.experimental.pallas import tpu as pltpu

F32 = jnp.float32
BF16 = jnp.bfloat16

EPS = 1e-6
POOL_WINDOWS = (2, 4, 8, 16)
POOL_GROUP = 128
POOL_WIDTH = 512
POOL_HALO = 16
DIL_PAIRS = ((128, 1), (512, 4), (2048, 16))
DIL_BAND = 128
DIL_HEADS_PER_GROUP = 4
DIL_HEADS = 12
DIL_HEAD_DIM = 64
DIL_GROUP_WIDTH = DIL_HEADS_PER_GROUP * DIL_HEAD_DIM
DIL_STEP_ROWS = 1024
DIL_BLOCKS_PER_GROUP = 4
REL_BUCKETS = 32
REL_MAX_DIST = 2048
SB_HEADS = 4
SB_HEAD_DIM = 128
SB_WIDTH = 512
SB_TILE = 256
S5_WIDTH = 512
S5_CH = 16
S5_GROUPS = 32
S5_STATE = 64
S5_CHUNKS = 4
S5_CHUNK_CH = S5_WIDTH // S5_CHUNKS
S5_CHUNK_STATE = (S5_GROUPS // S5_CHUNKS) * S5_STATE
BRANCH_ROWS = (0, 512, 768, 1280, 1792)
NEG_INF = -1e30
SB_UNDERFLOW = 160.0
LANES = 128
LOG2E = math.log2(math.e)

VMEM_LIMIT = 56 * 1024 * 1024


def _cparams(n_axes):
    return pltpu.CompilerParams(dimension_semantics=("arbitrary",) * n_axes,
                                vmem_limit_bytes=VMEM_LIMIT)


def _dot(a, b):
    return jnp.dot(a, b, preferred_element_type=F32)


def _dot_nt(a, b):
    return lax.dot_general(a, b, (((1,), (1,)), ((), ())), preferred_element_type=F32)


def _rms(x, g):
    ms = jnp.mean(x * x, axis=-1, keepdims=True)
    return x * lax.rsqrt(ms + EPS) * g


def _in_proj_kernel(x_ref, g_ref, wpool_ref, wd0_ref, wd1_ref, wd2_ref, wq_ref, wk_ref, wvt_ref,
                    ws5_ref, pool_ref, d0_ref, d1_ref, d2_ref, q_ref, k_ref, vt_ref, s5_ref,
                    scr_ref, *, tm, sb_scale):
    xn = _rms(x_ref[...], g_ref[...]).astype(BF16)
    pool_ref[...] = _dot(xn, wpool_ref[...])
    s5_ref[...] = _dot(xn, ws5_ref[...])

    def dil_proj(w_ref):
        q = _dot(xn, w_ref[:, :DIL_GROUP_WIDTH]) * (1.0 / math.sqrt(DIL_HEAD_DIM))
        kv = _dot(xn, w_ref[:, DIL_GROUP_WIDTH:])
        return q, kv

    q0, kv0 = dil_proj(wd0_ref)
    d0_ref[:, :DIL_GROUP_WIDTH] = q0.astype(BF16)
    d0_ref[:, DIL_GROUP_WIDTH:] = kv0.astype(BF16)
    for w_ref, out_ref, dil in ((wd1_ref, d1_ref, DIL_PAIRS[1][1]), (wd2_ref, d2_ref, DIL_PAIRS[2][1])):
        qg, kvg = dil_proj(w_ref)
        qkv = jnp.concatenate([qg, kvg], axis=1)
        for sl in range(qkv.shape[1] // LANES):
            scr_ref[sl] = qkv[:, sl * LANES:(sl + 1) * LANES]
        for r in range(dil):
            for sl in range(qkv.shape[1] // LANES):
                out_ref[r, :, sl * LANES:(sl + 1) * LANES] = (
                    scr_ref[sl, pl.ds(r, tm // dil, stride=dil), :].astype(BF16))

    q_ref[...] = (_dot(xn, wq_ref[...]) * sb_scale).astype(BF16)
    k_ref[...] = _dot(xn, wk_ref[...]).astype(BF16)
    vt = _dot_nt(wvt_ref[...], xn).astype(BF16)
    for jb in range(tm // SB_TILE):
        vt_ref[jb] = vt[:, jb * SB_TILE:(jb + 1) * SB_TILE]


def _in_proj(x, g, wpool, wd0, wd1, wd2, wq, wk, wvt, ws5, *, tm):
    b, s, d = x.shape
    dil1, dil2 = DIL_PAIRS[1][1], DIL_PAIRS[2][1]
    gw = 3 * DIL_GROUP_WIDTH
    const = lambda shape: pl.BlockSpec(shape, lambda bi, ti: (0,) * len(shape), pipeline_mode=pl.Buffered(1))
    out_shape = (
        jax.ShapeDtypeStruct((b, s, POOL_WIDTH), F32),
        jax.ShapeDtypeStruct((b, s, gw), BF16),
        jax.ShapeDtypeStruct((b, dil1, s // dil1, gw), BF16),
        jax.ShapeDtypeStruct((b, dil2, s // dil2, gw), BF16),
        jax.ShapeDtypeStruct((b, s, SB_WIDTH), BF16),
        jax.ShapeDtypeStruct((b, s, SB_WIDTH), BF16),
        jax.ShapeDtypeStruct((b, s // SB_TILE, SB_WIDTH, SB_TILE), BF16),
        jax.ShapeDtypeStruct((b, s, S5_WIDTH), F32),
    )
    out_specs = (
        pl.BlockSpec((None, tm, POOL_WIDTH), lambda bi, ti: (bi, ti, 0)),
        pl.BlockSpec((None, tm, gw), lambda bi, ti: (bi, ti, 0)),
        pl.BlockSpec((None, dil1, tm // dil1, gw), lambda bi, ti: (bi, 0, ti, 0)),
        pl.BlockSpec((None, dil2, tm // dil2, gw), lambda bi, ti: (bi, 0, ti, 0)),
        pl.BlockSpec((None, tm, SB_WIDTH), lambda bi, ti: (bi, ti, 0)),
        pl.BlockSpec((None, tm, SB_WIDTH), lambda bi, ti: (bi, ti, 0)),
        pl.BlockSpec((None, tm // SB_TILE, SB_WIDTH, SB_TILE), lambda bi, ti: (bi, ti, 0, 0)),
        pl.BlockSpec((None, tm, S5_WIDTH), lambda bi, ti: (bi, ti, 0)),
    )
    in_specs = [
        pl.BlockSpec((None, tm, d), lambda bi, ti: (bi, ti, 0)),
        const(g.shape), const(wpool.shape), const(wd0.shape), const(wd1.shape), const(wd2.shape),
        const(wq.shape), const(wk.shape), const(wvt.shape), const(ws5.shape),
    ]
    sb_scale = LOG2E / math.sqrt(SB_HEAD_DIM)
    return pl.pallas_call(
        functools.partial(_in_proj_kernel, tm=tm, sb_scale=sb_scale),
        grid=(b, s // tm), in_specs=in_specs, out_specs=out_specs, out_shape=out_shape,
        scratch_shapes=[pltpu.VMEM((gw // LANES, tm, LANES), F32)],
        compiler_params=_cparams(2), name="in_proj",
    )(x, g, wpool, wd0, wd1, wd2, wq, wk, wvt, ws5)


def _dil_attn_kernel(cur_ref, halo_ref, bias_ref, ones_ref, o_ref, lse_ref):
    nres, rows, _ = cur_ref.shape
    nblk = rows // DIL_BAND
    gw = DIL_GROUP_WIDTH
    nh = DIL_HEADS_PER_GROUP
    step = pl.program_id(2)
    lane_head = lax.broadcasted_iota(jnp.int32, (1, gw), 1) // DIL_HEAD_DIM
    hms = [lane_head == h for h in range(nh)]
    hmuls = [hm.astype(BF16) for hm in hms]
    ones_bd = ones_ref[...]

    def blocks(items):
        r0s = [n * DIL_BAND if isinstance(n, int) else pl.multiple_of(n * DIL_BAND, DIL_BAND) for _, n in items]
        kvs = []
        for (res, n), r0 in zip(items, r0s):
            if isinstance(n, int) and n == 0:
                kvs.append(jnp.concatenate([halo_ref[res, :, gw:], cur_ref[res, :DIL_BAND, gw:]], axis=0))
            else:
                kvs.append(cur_ref[res, pl.ds(r0 - DIL_BAND, 2 * DIL_BAND), gw:])
        ss = []
        for (res, _), r0, kvb in zip(items, r0s, kvs):
            q = cur_ref[res, pl.ds(r0, DIL_BAND), :gw]
            qs = jnp.concatenate([q * hmul for hmul in hmuls], axis=0)
            ss.append(_dot_nt(qs, kvb[:, :gw]))
        ps, ms = [], []
        for (_, n), s in zip(items, ss):
            if isinstance(n, int) and n > 0:
                first = 0
            else:
                first = (step * nblk + n == 0).astype(jnp.int32)
            p_h, m_h = [], []
            for h in range(nh):
                sh = s[h * DIL_BAND:(h + 1) * DIL_BAND, :] + bias_ref[first, h]
                m = jnp.max(sh, axis=-1, keepdims=True)
                p_h.append(jnp.exp(sh - m).astype(BF16))
                m_h.append(m)
            ps.append(jnp.concatenate(p_h, axis=1))
            ms.append(m_h)
        accs, dens = [], []
        for p, kvb in zip(ps, kvs):
            v = kvb[:, gw:]
            v_bd = jnp.concatenate([v * hmul for hmul in hmuls], axis=0)
            accs.append(_dot(p, v_bd))
            dens.append(_dot(p, ones_bd))
        for (res, _), r0, acc, den, m_h in zip(items, r0s, accs, dens, ms):
            m_b = m_h[0]
            for h in range(1, nh):
                m_b = jnp.where(hms[h], m_h[h], m_b)
            o_ref[res, pl.ds(r0, DIL_BAND), :] = acc / den
            lse_ref[res, pl.ds(r0, DIL_BAND), :] = m_b + jnp.log(den)

    group = DIL_BLOCKS_PER_GROUP
    if nres > 1:
        items = [(res, n) for res in range(nres) for n in range(nblk)]
        for g0 in range(0, len(items), group):
            blocks(items[g0:g0 + group])
    else:
        blocks([(0, n) for n in range(group)])

        def body(g, carry):
            blocks([(0, g * group + u) for u in range(group)])
            return carry

        lax.fori_loop(1, nblk // group, body, 0)


def _dil_attn(qkv, bias, ones_bd):
    b, dil, length, gw3 = qkv.shape
    rows = min(length, DIL_STEP_ROWS)
    nres = DIL_STEP_ROWS // rows
    rb = rows // DIL_BAND
    out_shape = (jax.ShapeDtypeStruct((b, dil, length, DIL_GROUP_WIDTH), F32),) * 2
    blk = lambda w: pl.BlockSpec((None, nres, rows, w), lambda bi, ri, i: (bi, ri, i, 0))
    halo = pl.BlockSpec((None, nres, DIL_BAND, gw3),
                        lambda bi, ri, i: (bi, ri, jnp.maximum(i * rb - 1, 0), 0))
    return pl.pallas_call(
        _dil_attn_kernel,
        grid=(b, dil // nres, length // rows),
        in_specs=[blk(gw3), halo, pl.BlockSpec(bias.shape, lambda bi, ri, i: (0, 0, 0, 0)),
                  pl.BlockSpec(ones_bd.shape, lambda bi, ri, i: (0, 0))],
        out_specs=(blk(DIL_GROUP_WIDTH), blk(DIL_GROUP_WIDTH)), out_shape=out_shape,
        compiler_params=_cparams(3), name=f"dil_attn_d{dil}",
    )(qkv, qkv, bias, ones_bd)


def _sb_attn_kernel(q_ref, k_ref, vt_ref, ntri_ref, o_ref, acc_ref):
    tk, tq = SB_TILE, q_ref.shape[0]
    i = pl.program_id(1)
    ntri = ntri_ref[...]
    row = lax.broadcasted_iota(jnp.int32, (tk, tq), 0)
    col = lax.broadcasted_iota(jnp.int32, (tk, tq), 1)
    sign = jnp.uint32(0x80000000)
    heads = range(SB_HEADS)
    hsl = [slice(h * SB_HEAD_DIM, (h + 1) * SB_HEAD_DIM) for h in heads]

    def tiles(specs, runs, first):
        chains = [(j, mask, h) for j, mask in specs for h in heads]
        kq = lambda j, h: _dot_nt(k_ref[pl.ds(pl.multiple_of(j * tk, tk), tk), hsl[h]], q_ref[:, hsl[h]])
        zs = [kq(j, h) for j, _, h in chains]
        his, z0s = [], []
        for (_, mask, _), z in zip(chains, zs):
            neg_abs = lax.bitcast_convert_type(lax.bitcast_convert_type(z, jnp.uint32) | sign, F32)
            sp = jnp.maximum(z, 0.0) + jnp.log(1.0 + jnp.exp2(neg_abs)) * LOG2E
            if mask is not None:
                sp = jnp.where(mask, sp, 0.0)
            his.append(sp.astype(BF16))
            z0s.append(z[0:1, :])
        ds = [z + _dot(ntri, hi) for z, hi in zip(zs, his)]
        runs = list(runs)
        ws = []
        for (_, mask, h), z0, d in zip(chains, z0s, ds):
            w = jnp.exp2(d - runs[h])
            if mask is not None:
                w = jnp.where(mask, w, 0.0)
            ws.append(w.astype(BF16))
            runs[h] = runs[h] + (z0 - d[0:1, :])
        pvs = [_dot(vt_ref[j, hsl[h], :], w) for (j, _, h), w in zip(chains, ws)]
        for h in heads:
            total = pvs[h]
            for t in range(1, len(specs)):
                total = total + pvs[t * SB_HEADS + h]
            if first:
                acc_ref[h] = total
            else:
                acc_ref[h] += total
        return tuple(runs)

    zero_runs = (jnp.zeros((1, tq), F32),) * SB_HEADS
    causal = row < col

    def min_run(runs):
        return jnp.min(jnp.minimum(jnp.minimum(runs[0], runs[1]), jnp.minimum(runs[2], runs[3])))

    @pl.when(i == 0)
    def _():
        tiles([(0, causal)], zero_runs, True)

    @pl.when(i > 0)
    def _():
        runs = tiles([(i, causal), (i - 1, None)], zero_runs, True)
        nfull = i - 1

        def cond(carry):
            jj, mr, _ = carry
            return jnp.logical_and(jj < nfull, mr < SB_UNDERFLOW)

        def body(carry):
            jj, _, runs = carry
            runs = tiles([(nfull - 1 - jj, None)], runs, False)
            return jj + 1, min_run(runs), runs

        lax.while_loop(cond, body, (jnp.int32(0), min_run(runs), runs))

    for h in heads:
        o_ref[:, hsl[h]] = acc_ref[h].T.astype(o_ref.dtype)


def _sb_attn(q, k, vt, ntri, *, tq):
    b, s, _ = q.shape
    tk = SB_TILE
    return pl.pallas_call(
        _sb_attn_kernel,
        grid=(b, s // tq),
        in_specs=[
            pl.BlockSpec((None, tq, SB_WIDTH), lambda bi, i: (bi, i, 0)),
            pl.BlockSpec((None, s, SB_WIDTH), lambda bi, i: (bi, 0, 0)),
            pl.BlockSpec((None, s // tk, SB_WIDTH, tk), lambda bi, i: (bi, 0, 0, 0)),
            pl.BlockSpec((tk, tk), lambda bi, i: (0, 0)),
        ],
        out_specs=pl.BlockSpec((None, tq, SB_WIDTH), lambda bi, i: (bi, i, 0)),
        out_shape=jax.ShapeDtypeStruct((b, s, SB_WIDTH), BF16),
        scratch_shapes=[pltpu.VMEM((SB_HEADS, SB_HEAD_DIM, tq), F32)],
        compiler_params=_cparams(2), name="sb_attn",
    )(q, k, vt, ntri)


def _gelu_tanh(x):
    return 0.5 * x * (1.0 + jnp.tanh(math.sqrt(2.0 / math.pi) * (x + 0.044715 * (x * x * x))))


def _s5_kernel(u_ref, lr_ref, li_ref, wb_ref, wc_ref, d_ref, wglu_ref, out_ref, bu_ref, h_ref, *, steps):
    cs = S5_CHUNK_STATE

    @pl.when(pl.program_id(0) == 0)
    def _():
        h_ref[...] = jnp.zeros_like(h_ref)

    nb = h_ref.shape[0]
    u = pltpu.einshape("btc->tbc", u_ref[...]).reshape(steps * nb, S5_WIDTH)
    ub = u.astype(BF16)
    for c in range(S5_CHUNKS):
        bu_ref[:, 2 * cs * c:2 * cs * (c + 1)] = _dot(ub[:, S5_CHUNK_CH * c:S5_CHUNK_CH * (c + 1)], wb_ref[c])

    for c0 in range(0, S5_CHUNKS, 2):
        lam = [(jnp.broadcast_to(lr_ref[c], (nb, cs)), jnp.broadcast_to(li_ref[c], (nb, cs)))
               for c in (c0, c0 + 1)]
        offs = [2 * cs * c for c in (c0, c0 + 1)]

        def step(t, carry):
            r = pl.multiple_of(t * nb, nb)
            new = []
            for (lr, li), off, (hr, hi) in zip(lam, offs, carry):
                br = bu_ref[pl.ds(r, nb), off:off + cs]
                bi = bu_ref[pl.ds(r, nb), off + cs:off + 2 * cs]
                nhr = lr * hr - li * hi + br
                nhi = lr * hi + li * hr + bi
                bu_ref[pl.ds(r, nb), off:off + cs] = nhr
                bu_ref[pl.ds(r, nb), off + cs:off + 2 * cs] = nhi
                new.append((nhr, nhi))
            return tuple(new)

        init = tuple((h_ref[:, off:off + cs], h_ref[:, off + cs:off + 2 * cs]) for off in offs)
        fin = lax.fori_loop(0, steps, step, init, unroll=2)
        for off, (hr, hi) in zip(offs, fin):
            h_ref[:, off:off + cs] = hr
            h_ref[:, off + cs:off + 2 * cs] = hi

    y = jnp.concatenate(
        [_dot(bu_ref[:, 2 * cs * c:2 * cs * (c + 1)].astype(BF16), wc_ref[c]) for c in range(S5_CHUNKS)],
        axis=1)
    y = _gelu_tanh(y + d_ref[...] * u)
    gl = _dot(y.astype(BF16), wglu_ref[...])
    out = gl[:, :S5_WIDTH] * jax.nn.sigmoid(gl[:, S5_WIDTH:])
    out_ref[...] = pltpu.einshape("tbc->btc", out.reshape(steps, nb, S5_WIDTH))


def _s5(u, lr, li, wb, wc, d, wglu, *, steps):
    batch, s, _ = u.shape
    rows = steps * batch
    const = lambda a: pl.BlockSpec(a.shape, lambda i: (0,) * a.ndim)
    blk = pl.BlockSpec((batch, steps, S5_WIDTH), lambda i: (0, i, 0))
    return pl.pallas_call(
        functools.partial(_s5_kernel, steps=steps),
        grid=(s // steps,),
        in_specs=[blk, const(lr), const(li), const(wb), const(wc), const(d), const(wglu)],
        out_specs=blk,
        out_shape=jax.ShapeDtypeStruct((batch, s, S5_WIDTH), F32),
        scratch_shapes=[pltpu.VMEM((rows, 2 * S5_CHUNKS * S5_CHUNK_STATE), F32),
                        pltpu.VMEM((batch, 2 * S5_CHUNKS * S5_CHUNK_STATE), F32)],
        compiler_params=_cparams(1), name="s5_mixer",
    )(u, lr, li, wb, wc, d, wglu)


def _merge_kernel(x_ref, g_ref, up_ref, o0_ref, l0_ref, o1_ref, l1_ref, o2_ref, l2_ref, ysb_ref, ys5_ref,
                  pw_ref, ps_ref, wg_ref, wb_ref, wo_ref, out_ref, ext_ref, no1_ref, nl1_ref, no2_ref, nl2_ref,
                  *, tm):
    ti = pl.program_id(1)
    x = x_ref[...]
    xn = _rms(x, g_ref[...]).astype(BF16)

    @pl.when(ti == 0)
    def _():
        ext_ref[:POOL_HALO, :] = jnp.zeros((POOL_HALO, POOL_WIDTH), F32)

    ext_ref[POOL_HALO:, :] = up_ref[...]
    pos = ti * tm + lax.broadcasted_iota(jnp.int32, (tm, 1), 0)
    ys = []
    for gi, w in enumerate(POOL_WINDOWS):
        c0, c1 = gi * POOL_GROUP, (gi + 1) * POOL_GROUP
        acc = ext_ref[pl.ds(POOL_HALO, tm), c0:c1]
        for j in range(1, w):
            acc = acc + ext_ref[pl.ds(POOL_HALO - j, tm), c0:c1]
        cnt = jnp.minimum(pos + 1, w).astype(F32)
        p = acc / cnt - ext_ref[pl.ds(POOL_HALO, tm), c0:c1]
        ys.append(_dot(p.astype(BF16), pw_ref[gi]))
    y_pool = jnp.concatenate(ys, axis=1) * ps_ref[...]
    ext_ref[:POOL_HALO, :] = ext_ref[pl.ds(tm, POOL_HALO), :]

    nslab = DIL_GROUP_WIDTH // LANES

    def to_token_order(src_ref, dst_ref):
        dil = src_ref.shape[0]
        for r in range(dil):
            for sl in range(nslab):
                dst_ref[sl, pl.ds(r, tm // dil, stride=dil), :] = src_ref[r, :, sl * LANES:(sl + 1) * LANES]
        return jnp.concatenate([dst_ref[sl] for sl in range(nslab)], axis=1)

    o1, l1 = to_token_order(o1_ref, no1_ref), to_token_order(l1_ref, nl1_ref)
    o2, l2 = to_token_order(o2_ref, no2_ref), to_token_order(l2_ref, nl2_ref)
    l0 = l0_ref[...]
    m = jnp.maximum(jnp.maximum(l0, l1), l2)
    e0, e1, e2 = jnp.exp(l0 - m), jnp.exp(l1 - m), jnp.exp(l2 - m)
    y_dil = (e0 * o0_ref[...] + e1 * o1 + e2 * o2) / (e0 + e1 + e2)

    branches = (y_pool.astype(BF16), y_dil.astype(BF16), ysb_ref[...], ys5_ref[...].astype(BF16))
    merged = jnp.zeros((tm, x.shape[1]), F32)
    for bi, yb in enumerate(branches):
        gate = jax.nn.sigmoid(_dot(xn, wg_ref[bi]))
        merged = merged + gate * _dot(yb, wb_ref[BRANCH_ROWS[bi]:BRANCH_ROWS[bi + 1], :])
    out_ref[...] = x + _dot(merged.astype(BF16), wo_ref[...])


def _merge(x, g, up, o0, l0, o1, l1, o2, l2, ysb, ys5, pw, ps, wg, wb, wo, *, tm):
    b, s, d = x.shape
    dil1, dil2 = o1.shape[1], o2.shape[1]
    gw = DIL_GROUP_WIDTH
    tok = lambda w: pl.BlockSpec((None, tm, w), lambda bi, ti: (bi, ti, 0))
    res = lambda dil: pl.BlockSpec((None, dil, tm // dil, gw), lambda bi, ti: (bi, 0, ti, 0))
    const = lambda a: pl.BlockSpec(a.shape, lambda bi, ti: (0,) * a.ndim)
    in_specs = [tok(d), const(g), tok(POOL_WIDTH), tok(gw), tok(gw), res(dil1), res(dil1), res(dil2), res(dil2),
                tok(SB_WIDTH), tok(S5_WIDTH),
                const(pw), const(ps), const(wg), const(wb), const(wo)]
    return pl.pallas_call(
        functools.partial(_merge_kernel, tm=tm),
        grid=(b, s // tm), in_specs=in_specs, out_specs=tok(d),
        out_shape=jax.ShapeDtypeStruct((b, s, d), F32),
        scratch_shapes=[pltpu.VMEM((tm + POOL_HALO, POOL_WIDTH), F32)]
        + [pltpu.VMEM((gw // LANES, tm, LANES), F32)] * 4,
        compiler_params=_cparams(2), name="gated_merge",
    )(x, g, up, o0, l0, o1, l1, o2, l2, ysb, ys5, pw, ps, wg, wb, wo)


def _ffn_kernel(x_ref, g_ref, wup_ref, wdn_ref, fg_ref, out_ref, *, hidden, final):
    x = x_ref[...]
    hn = _rms(x, g_ref[...]).astype(BF16)
    gu = _dot(hn, wup_ref[...])
    h = (jax.nn.silu(gu[:, :hidden]) * gu[:, hidden:]).astype(BF16)
    y = x + _dot(h, wdn_ref[...])
    if final:
        y = _rms(y, fg_ref[...])
    out_ref[...] = y


def _ffn(x2d, g, wup, wdn, fg, *, tm, final):
    n, d = x2d.shape
    hidden = wdn.shape[0]
    const = lambda a: pl.BlockSpec(a.shape, lambda i: (0,) * a.ndim, pipeline_mode=pl.Buffered(1))
    return pl.pallas_call(
        functools.partial(_ffn_kernel, hidden=hidden, final=final),
        grid=(n // tm,),
        in_specs=[pl.BlockSpec((tm, d), lambda i: (i, 0)), const(g), const(wup), const(wdn), const(fg)],
        out_specs=pl.BlockSpec((tm, d), lambda i: (i, 0)),
        out_shape=jax.ShapeDtypeStruct((n, d), F32),
        compiler_params=_cparams(1), name="swiglu_final" if final else "swiglu",
    )(x2d, g, wup, wdn, fg)


def _t5_bucket(dist):
    exact = REL_BUCKETS // 2
    df = jnp.maximum(dist, 1).astype(F32)
    large = exact + (jnp.log(df / exact) / math.log(REL_MAX_DIST / exact)
                     * (REL_BUCKETS - exact)).astype(jnp.int32)
    large = jnp.minimum(large, REL_BUCKETS - 1)
    return jnp.where(dist < exact, dist, large)


def _dil_bias(rel_bias_g, dil):
    band = DIL_BAND
    i = jnp.arange(band)[:, None]
    c = jnp.arange(2 * band)[None, :]
    dist_sub = band + i - c
    in_band = (dist_sub >= 0) & (dist_sub <= band)
    buckets = _t5_bucket(jnp.clip(dist_sub, 0, band) * dil)
    onehot = (buckets[None] == jnp.arange(REL_BUCKETS)[:, None, None]).astype(F32)
    bias = jnp.einsum('kh,kic->hic', rel_bias_g.astype(F32), onehot, precision=lax.Precision.HIGHEST)
    bias = jnp.where(in_band[None], bias, NEG_INF)
    first = jnp.where((c >= band)[None], bias, NEG_INF)
    return jnp.stack([bias, first])


def _dil_weights(w_dil, group):
    hw = DIL_HEADS * DIL_HEAD_DIM
    cols = [w_dil[:, i * hw + group * DIL_GROUP_WIDTH:i * hw + (group + 1) * DIL_GROUP_WIDTH] for i in range(3)]
    return jnp.concatenate(cols, axis=1).astype(BF16)


def _s5_params(a_re, a_im, log_dt, b_re, b_im, c_re, c_im):
    lam = lax.complex(a_re.astype(F32), a_im.astype(F32))
    dt = jnp.exp(log_dt.astype(F32))[:, None]
    lam_bar = jnp.exp(lam * dt)
    b_bar = ((lam_bar - 1.0) / lam)[:, :, None] * lax.complex(b_re.astype(F32), b_im.astype(F32))
    gpc = S5_GROUPS // S5_CHUNKS
    eye = jnp.eye(gpc, dtype=F32)

    def pack_b(bm):
        bm = bm.reshape(S5_CHUNKS, gpc, S5_STATE, S5_CH)
        return jnp.einsum('cgpk,gh->cgkhp', bm, eye).reshape(S5_CHUNKS, S5_CHUNK_CH, S5_CHUNK_STATE)

    def pack_c(cm):
        cm = cm.reshape(S5_CHUNKS, gpc, S5_CH, S5_STATE)
        return jnp.einsum('cgkp,gh->cgphk', cm, eye).reshape(S5_CHUNKS, S5_CHUNK_STATE, S5_CHUNK_CH)

    wb = jnp.concatenate([pack_b(jnp.real(b_bar)), pack_b(jnp.imag(b_bar))], axis=2).astype(BF16)
    wc = jnp.concatenate([pack_c(c_re.astype(F32)), pack_c(-c_im.astype(F32))], axis=1).astype(BF16)
    lr = jnp.real(lam_bar).reshape(S5_CHUNKS, 1, S5_CHUNK_STATE)
    li = jnp.imag(lam_bar).reshape(S5_CHUNKS, 1, S5_CHUNK_STATE)
    return lr, li, wb, wc


def kernel(x, attn_norm_g, w_in, pool_w, pool_scale, rel_bias, s5_a_re, s5_a_im, s5_log_dt, s5_b_re, s5_b_im,
           s5_c_re, s5_c_im, s5_d, s5_w_glu, w_branch, w_gate, w_out, ffn_norm_g, w_up, w_down, final_norm_g):
    b, s, d = x.shape
    depth = w_in.shape[0]
    tm = 1024
    o1 = POOL_WIDTH
    o2 = o1 + 3 * DIL_HEADS * DIL_HEAD_DIM
    o3 = o2 + 3 * SB_WIDTH

    t = SB_TILE
    ntri = -(jnp.arange(t)[None, :] >= jnp.arange(t)[:, None]).astype(BF16)
    biases = [_dil_bias(rel_bias[:, g * DIL_HEADS_PER_GROUP:(g + 1) * DIL_HEADS_PER_GROUP], dil)
              for g, (_, dil) in enumerate(DIL_PAIRS)]
    key_head = jnp.arange(DIL_HEADS_PER_GROUP * 2 * DIL_BAND) // (2 * DIL_BAND)
    ones_bd = (key_head[:, None] == (jnp.arange(DIL_GROUP_WIDTH) // DIL_HEAD_DIM)[None, :]).astype(BF16)
    fg = final_norm_g.reshape(1, d).astype(F32)

    for l in range(depth):
        wl = w_in[l]
        g_attn = attn_norm_g[l].reshape(1, d).astype(F32)
        w_dil = wl[:, o1:o2]
        w_sb = wl[:, o2:o3]
        up, d0, d1, d2, q, k, vt, u_s5 = _in_proj(
            x, g_attn, wl[:, :o1].astype(BF16),
            _dil_weights(w_dil, 0), _dil_weights(w_dil, 1), _dil_weights(w_dil, 2),
            w_sb[:, :SB_WIDTH].astype(BF16), w_sb[:, SB_WIDTH:2 * SB_WIDTH].astype(BF16),
            w_sb[:, 2 * SB_WIDTH:].T.astype(BF16), wl[:, o3:].astype(BF16), tm=tm)

        oa0, la0 = _dil_attn(d0.reshape(b, 1, s, d0.shape[-1]), biases[0], ones_bd)
        oa1, la1 = _dil_attn(d1, biases[1], ones_bd)
        oa2, la2 = _dil_attn(d2, biases[2], ones_bd)
        y_sb = _sb_attn(q, k, vt, ntri, tq=256)

        lr, li, wb, wc = _s5_params(s5_a_re[l], s5_a_im[l], s5_log_dt[l], s5_b_re[l], s5_b_im[l],
                                    s5_c_re[l], s5_c_im[l])
        y_s5 = _s5(u_s5, lr, li, wb, wc, s5_d[l].reshape(1, S5_WIDTH).astype(F32), s5_w_glu[l].astype(BF16),
                   steps=128)

        x = _merge(x, g_attn, up, oa0.reshape(b, s, -1), la0.reshape(b, s, -1), oa1, la1, oa2, la2,
                   y_sb, y_s5,
                   pool_w[l].astype(BF16), pool_scale[l].reshape(1, POOL_WIDTH).astype(F32),
                   w_gate[l].astype(BF16), w_branch[l].astype(BF16), w_out[l].astype(BF16), tm=256)

        x = _ffn(x.reshape(b * s, d), ffn_norm_g[l].reshape(1, d).astype(F32), w_up[l].astype(BF16),
                 w_down[l].astype(BF16), fg, tm=512, final=(l == depth - 1)).reshape(b, s, d)
    return x
```

```python
import functools
import math

import jax
import jax.numpy as jnp
from jax import lax
from jax.experimental import pallas as pl
from jax.experimental.pallas import tpu as pltpu

F32 = jnp.float32
BF16 = jnp.bfloat16

EPS = 1e-6
POOL_WINDOWS = (2, 4, 8, 16)
POOL_GROUP = 128
POOL_WIDTH = 512
POOL_HALO = 16
DIL_PAIRS = ((128, 1), (512, 4), (2048, 16))
DIL_BAND = 128
DIL_HEADS_PER_GROUP = 4
DIL_HEADS = 12
DIL_HEAD_DIM = 64
DIL_GROUP_WIDTH = DIL_HEADS_PER_GROUP * DIL_HEAD_DIM
DIL_STEP_ROWS = 1024
DIL_BLOCKS_PER_GROUP = 4
REL_BUCKETS = 32
REL_MAX_DIST = 2048
SB_HEADS = 4
SB_HEAD_DIM = 128
SB_WIDTH = 512
SB_TILE = 256
S5_WIDTH = 512
S5_CH = 16
S5_GROUPS = 32
S5_STATE = 64
S5_CHUNKS = 4
S5_CHUNK_CH = S5_WIDTH // S5_CHUNKS
S5_CHUNK_STATE = (S5_GROUPS // S5_CHUNKS) * S5_STATE
BRANCH_ROWS = (0, 512, 768, 1280, 1792)
NEG_INF = -1e30
SB_UNDERFLOW = 160.0
LANES = 128
LOG2E = math.log2(math.e)

IN_PROJ_ROWS = 1024
MERGE_ROWS = 256
FFN_ROWS = 512
S5_STEPS = 128

VMEM_LIMIT = 56 * 1024 * 1024


def _cparams(n_axes):
    return pltpu.CompilerParams(dimension_semantics=("arbitrary",) * n_axes,
                                vmem_limit_bytes=VMEM_LIMIT)


def _dot(a, b):
    return jnp.dot(a, b, preferred_element_type=F32)


def _dot_nt(a, b):
    return lax.dot_general(a, b, (((1,), (1,)), ((), ())), preferred_element_type=F32)


def _rms(x, g):
    ms = jnp.mean(x * x, axis=-1, keepdims=True)
    return x * lax.rsqrt(ms + EPS) * g


def _in_proj_kernel(x_ref, g_ref, wpool_ref, wd0_ref, wd1_ref, wd2_ref, wq_ref, wk_ref, wvt_ref,
                    ws5_ref, pool_ref, d0_ref, d1_ref, d2_ref, q_ref, k_ref, vt_ref, s5_ref,
                    scr_ref, *, tm, sb_scale):
    xn = _rms(x_ref[...], g_ref[...]).astype(BF16)
    pool_ref[...] = _dot(xn, wpool_ref[...])
    s5_ref[...] = _dot(xn, ws5_ref[...])

    def dil_proj(w_ref):
        q = _dot(xn, w_ref[:, :DIL_GROUP_WIDTH]) * (1.0 / math.sqrt(DIL_HEAD_DIM))
        kv = _dot(xn, w_ref[:, DIL_GROUP_WIDTH:])
        return q, kv

    q0, kv0 = dil_proj(wd0_ref)
    d0_ref[:, :DIL_GROUP_WIDTH] = q0.astype(BF16)
    d0_ref[:, DIL_GROUP_WIDTH:] = kv0.astype(BF16)
    for w_ref, out_ref, dil in ((wd1_ref, d1_ref, DIL_PAIRS[1][1]), (wd2_ref, d2_ref, DIL_PAIRS[2][1])):
        qg, kvg = dil_proj(w_ref)
        qkv = jnp.concatenate([qg, kvg], axis=1)
        for sl in range(qkv.shape[1] // LANES):
            scr_ref[sl] = qkv[:, sl * LANES:(sl + 1) * LANES]
        for r in range(dil):
            for sl in range(qkv.shape[1] // LANES):
                out_ref[r, :, sl * LANES:(sl + 1) * LANES] = (
                    scr_ref[sl, pl.ds(r, tm // dil, stride=dil), :].astype(BF16))

    q_ref[...] = (_dot(xn, wq_ref[...]) * sb_scale).astype(BF16)
    k_ref[...] = _dot(xn, wk_ref[...]).astype(BF16)
    vt = _dot_nt(wvt_ref[...], xn).astype(BF16)
    for jb in range(tm // SB_TILE):
        vt_ref[jb] = vt[:, jb * SB_TILE:(jb + 1) * SB_TILE]


def _in_proj(x, g, wpool, wd0, wd1, wd2, wq, wk, wvt, ws5, *, tm):
    b, s, d = x.shape
    dil1, dil2 = DIL_PAIRS[1][1], DIL_PAIRS[2][1]
    gw = 3 * DIL_GROUP_WIDTH
    const = lambda shape: pl.BlockSpec(shape, lambda bi, ti: (0,) * len(shape), pipeline_mode=pl.Buffered(1))
    out_shape = (
        jax.ShapeDtypeStruct((b, s, POOL_WIDTH), F32),
        jax.ShapeDtypeStruct((b, s, gw), BF16),
        jax.ShapeDtypeStruct((b, dil1, s // dil1, gw), BF16),
        jax.ShapeDtypeStruct((b, dil2, s // dil2, gw), BF16),
        jax.ShapeDtypeStruct((b, s, SB_WIDTH), BF16),
        jax.ShapeDtypeStruct((b, s, SB_WIDTH), BF16),
        jax.ShapeDtypeStruct((b, s // SB_TILE, SB_WIDTH, SB_TILE), BF16),
        jax.ShapeDtypeStruct((b, s, S5_WIDTH), F32),
    )
    out_specs = (
        pl.BlockSpec((None, tm, POOL_WIDTH), lambda bi, ti: (bi, ti, 0)),
        pl.BlockSpec((None, tm, gw), lambda bi, ti: (bi, ti, 0)),
        pl.BlockSpec((None, dil1, tm // dil1, gw), lambda bi, ti: (bi, 0, ti, 0)),
        pl.BlockSpec((None, dil2, tm // dil2, gw), lambda bi, ti: (bi, 0, ti, 0)),
        pl.BlockSpec((None, tm, SB_WIDTH), lambda bi, ti: (bi, ti, 0)),
        pl.BlockSpec((None, tm, SB_WIDTH), lambda bi, ti: (bi, ti, 0)),
        pl.BlockSpec((None, tm // SB_TILE, SB_WIDTH, SB_TILE), lambda bi, ti: (bi, ti, 0, 0)),
        pl.BlockSpec((None, tm, S5_WIDTH), lambda bi, ti: (bi, ti, 0)),
    )
    in_specs = [
        pl.BlockSpec((None, tm, d), lambda bi, ti: (bi, ti, 0)),
        const(g.shape), const(wpool.shape), const(wd0.shape), const(wd1.shape), const(wd2.shape),
        const(wq.shape), const(wk.shape), const(wvt.shape), const(ws5.shape),
    ]
    sb_scale = LOG2E / math.sqrt(SB_HEAD_DIM)
    return pl.pallas_call(
        functools.partial(_in_proj_kernel, tm=tm, sb_scale=sb_scale),
        grid=(b, s // tm), in_specs=in_specs, out_specs=out_specs, out_shape=out_shape,
        scratch_shapes=[pltpu.VMEM((gw // LANES, tm, LANES), F32)],
        compiler_params=_cparams(2), name="in_proj",
    )(x, g, wpool, wd0, wd1, wd2, wq, wk, wvt, ws5)


def _dil_attn_kernel(cur_ref, halo_ref, bias_ref, ones_ref, o_ref, lse_ref):
    nres, rows, _ = cur_ref.shape
    nblk = rows // DIL_BAND
    gw = DIL_GROUP_WIDTH
    nh = DIL_HEADS_PER_GROUP
    step = pl.program_id(2)
    lane_head = lax.broadcasted_iota(jnp.int32, (1, gw), 1) // DIL_HEAD_DIM
    hms = [lane_head == h for h in range(nh)]
    hmuls = [hm.astype(BF16) for hm in hms]
    ones_bd = ones_ref[...]

    def blocks(items):
        r0s = [n * DIL_BAND if isinstance(n, int) else pl.multiple_of(n * DIL_BAND, DIL_BAND) for _, n in items]
        kvs = []
        for (res, n), r0 in zip(items, r0s):
            if isinstance(n, int) and n == 0:
                kvs.append(jnp.concatenate([halo_ref[res, :, gw:], cur_ref[res, :DIL_BAND, gw:]], axis=0))
            else:
                kvs.append(cur_ref[res, pl.ds(r0 - DIL_BAND, 2 * DIL_BAND), gw:])
        ss = []
        for (res, _), r0, kvb in zip(items, r0s, kvs):
            q = cur_ref[res, pl.ds(r0, DIL_BAND), :gw]
            qs = jnp.concatenate([q * hmul for hmul in hmuls], axis=0)
            ss.append(_dot_nt(qs, kvb[:, :gw]))
        ps, ms = [], []
        for (_, n), s in zip(items, ss):
            if isinstance(n, int) and n > 0:
                first = 0
            else:
                first = (step * nblk + n == 0).astype(jnp.int32)
            p_h, m_h = [], []
            for h in range(nh):
                sh = s[h * DIL_BAND:(h + 1) * DIL_BAND, :] + bias_ref[first, h]
                m = jnp.max(sh, axis=-1, keepdims=True)
                p_h.append(jnp.exp(sh - m).astype(BF16))
                m_h.append(m)
            ps.append(jnp.concatenate(p_h, axis=1))
            ms.append(m_h)
        accs, dens = [], []
        for p, kvb in zip(ps, kvs):
            v = kvb[:, gw:]
            v_bd = jnp.concatenate([v * hmul for hmul in hmuls], axis=0)
            accs.append(_dot(p, v_bd))
            dens.append(_dot(p, ones_bd))
        for (res, _), r0, acc, den, m_h in zip(items, r0s, accs, dens, ms):
            m_b = m_h[0]
            for h in range(1, nh):
                m_b = jnp.where(hms[h], m_h[h], m_b)
            o_ref[res, pl.ds(r0, DIL_BAND), :] = acc / den
            lse_ref[res, pl.ds(r0, DIL_BAND), :] = m_b + jnp.log(den)

    group = DIL_BLOCKS_PER_GROUP
    if nres > 1:
        items = [(res, n) for res in range(nres) for n in range(nblk)]
        for g0 in range(0, len(items), group):
            blocks(items[g0:g0 + group])
    else:
        blocks([(0, n) for n in range(group)])

        def body(g, carry):
            blocks([(0, g * group + u) for u in range(group)])
            return carry

        lax.fori_loop(1, nblk // group, body, 0)


def _dil_attn(qkv, bias, ones_bd):
    b, dil, length, gw3 = qkv.shape
    rows = min(length, DIL_STEP_ROWS)
    nres = DIL_STEP_ROWS // rows
    rb = rows // DIL_BAND
    out_shape = (jax.ShapeDtypeStruct((b, dil, length, DIL_GROUP_WIDTH), F32),) * 2
    blk = lambda w: pl.BlockSpec((None, nres, rows, w), lambda bi, ri, i: (bi, ri, i, 0))
    halo = pl.BlockSpec((None, nres, DIL_BAND, gw3),
                        lambda bi, ri, i: (bi, ri, jnp.maximum(i * rb - 1, 0), 0))
    return pl.pallas_call(
        _dil_attn_kernel,
        grid=(b, dil // nres, length // rows),
        in_specs=[blk(gw3), halo, pl.BlockSpec(bias.shape, lambda bi, ri, i: (0, 0, 0, 0)),
                  pl.BlockSpec(ones_bd.shape, lambda bi, ri, i: (0, 0))],
        out_specs=(blk(DIL_GROUP_WIDTH), blk(DIL_GROUP_WIDTH)), out_shape=out_shape,
        compiler_params=_cparams(3), name=f"dil_attn_d{dil}",
    )(qkv, qkv, bias, ones_bd)


def _sb_attn_kernel(q_ref, k_ref, vt_ref, ntri_ref, o_ref, acc_ref):
    tk = tq = SB_TILE
    i = pl.program_id(1)
    ntri = ntri_ref[...]
    row = lax.broadcasted_iota(jnp.int32, (tk, tq), 0)
    col = lax.broadcasted_iota(jnp.int32, (tk, tq), 1)
    sign = jnp.uint32(0x80000000)
    heads = range(SB_HEADS)
    hsl = [slice(h * SB_HEAD_DIM, (h + 1) * SB_HEAD_DIM) for h in heads]

    def tiles(specs, runs, first):
        chains = [(j, mask, h) for j, mask in specs for h in heads]
        kq = lambda j, h: _dot_nt(k_ref[pl.ds(pl.multiple_of(j * tk, tk), tk), hsl[h]], q_ref[:, hsl[h]])
        zs = [kq(j, h) for j, _, h in chains]
        his, z0s = [], []
        for (_, mask, _), z in zip(chains, zs):
            neg_abs = lax.bitcast_convert_type(lax.bitcast_convert_type(z, jnp.uint32) | sign, F32)
            sp = jnp.maximum(z, 0.0) + jnp.log(1.0 + jnp.exp2(neg_abs)) * LOG2E
            if mask is not None:
                sp = jnp.where(mask, sp, 0.0)
            his.append(sp.astype(BF16))
            z0s.append(z[0:1, :])
        ds = [z + _dot(ntri, hi) for z, hi in zip(zs, his)]
        runs = list(runs)
        ws = []
        for (_, mask, h), z0, d in zip(chains, z0s, ds):
            w = jnp.exp2(d - runs[h])
            if mask is not None:
                w = jnp.where(mask, w, 0.0)
            ws.append(w.astype(BF16))
            runs[h] = runs[h] + (z0 - d[0:1, :])
        pvs = [_dot(vt_ref[j, hsl[h], :], w) for (j, _, h), w in zip(chains, ws)]
        for h in heads:
            total = pvs[h]
            for t in range(1, len(specs)):
                total = total + pvs[t * SB_HEADS + h]
            if first:
                acc_ref[h] = total
            else:
                acc_ref[h] += total
        return tuple(runs)

    zero_runs = (jnp.zeros((1, tq), F32),) * SB_HEADS
    causal = row < col

    def min_run(runs):
        return jnp.min(jnp.minimum(jnp.minimum(runs[0], runs[1]), jnp.minimum(runs[2], runs[3])))

    @pl.when(i == 0)
    def _():
        tiles([(0, causal)], zero_runs, True)

    @pl.when(i > 0)
    def _():
        runs = tiles([(i, causal), (i - 1, None)], zero_runs, True)
        nfull = i - 1

        def cond(carry):
            jj, mr, _ = carry
            return jnp.logical_and(jj < nfull, mr < SB_UNDERFLOW)

        def body(carry):
            jj, _, runs = carry
            runs = tiles([(nfull - 1 - jj, None)], runs, False)
            return jj + 1, min_run(runs), runs

        lax.while_loop(cond, body, (jnp.int32(0), min_run(runs), runs))

    for h in heads:
        o_ref[:, hsl[h]] = acc_ref[h].T.astype(o_ref.dtype)


def _sb_attn(q, k, vt, ntri):
    b, s, _ = q.shape
    tq = tk = SB_TILE
    return pl.pallas_call(
        _sb_attn_kernel,
        grid=(b, s // tq),
        in_specs=[
            pl.BlockSpec((None, tq, SB_WIDTH), lambda bi, i: (bi, i, 0)),
            pl.BlockSpec((None, s, SB_WIDTH), lambda bi, i: (bi, 0, 0)),
            pl.BlockSpec((None, s // tk, SB_WIDTH, tk), lambda bi, i: (bi, 0, 0, 0)),
            pl.BlockSpec((tk, tk), lambda bi, i: (0, 0)),
        ],
        out_specs=pl.BlockSpec((None, tq, SB_WIDTH), lambda bi, i: (bi, i, 0)),
        out_shape=jax.ShapeDtypeStruct((b, s, SB_WIDTH), BF16),
        scratch_shapes=[pltpu.VMEM((SB_HEADS, SB_HEAD_DIM, tq), F32)],
        compiler_params=_cparams(2), name="sb_attn",
    )(q, k, vt, ntri)


def _gelu_tanh(x):
    return 0.5 * x * (1.0 + jnp.tanh(math.sqrt(2.0 / math.pi) * (x + 0.044715 * (x * x * x))))


def _s5_kernel(u_ref, lr_ref, li_ref, wb_ref, wc_ref, d_ref, wglu_ref, out_ref, bu_ref, h_ref, *, steps):
    cs = S5_CHUNK_STATE

    @pl.when(pl.program_id(0) == 0)
    def _():
        h_ref[...] = jnp.zeros_like(h_ref)

    nb = h_ref.shape[0]
    u = pltpu.einshape("btc->tbc", u_ref[...]).reshape(steps * nb, S5_WIDTH)
    ub = u.astype(BF16)
    for c in range(S5_CHUNKS):
        bu_ref[:, 2 * cs * c:2 * cs * (c + 1)] = _dot(ub[:, S5_CHUNK_CH * c:S5_CHUNK_CH * (c + 1)], wb_ref[c])

    for c0 in range(0, S5_CHUNKS, 2):
        lam = [(jnp.broadcast_to(lr_ref[c], (nb, cs)), jnp.broadcast_to(li_ref[c], (nb, cs)))
               for c in (c0, c0 + 1)]
        offs = [2 * cs * c for c in (c0, c0 + 1)]

        def step(t, carry):
            r = pl.multiple_of(t * nb, nb)
            new = []
            for (lr, li), off, (hr, hi) in zip(lam, offs, carry):
                br = bu_ref[pl.ds(r, nb), off:off + cs]
                bi = bu_ref[pl.ds(r, nb), off + cs:off + 2 * cs]
                nhr = lr * hr - li * hi + br
                nhi = lr * hi + li * hr + bi
                bu_ref[pl.ds(r, nb), off:off + cs] = nhr
                bu_ref[pl.ds(r, nb), off + cs:off + 2 * cs] = nhi
                new.append((nhr, nhi))
            return tuple(new)

        init = tuple((h_ref[:, off:off + cs], h_ref[:, off + cs:off + 2 * cs]) for off in offs)
        fin = lax.fori_loop(0, steps, step, init, unroll=2)
        for off, (hr, hi) in zip(offs, fin):
            h_ref[:, off:off + cs] = hr
            h_ref[:, off + cs:off + 2 * cs] = hi

    y = jnp.concatenate(
        [_dot(bu_ref[:, 2 * cs * c:2 * cs * (c + 1)].astype(BF16), wc_ref[c]) for c in range(S5_CHUNKS)],
        axis=1)
    y = _gelu_tanh(y + d_ref[...] * u)
    gl = _dot(y.astype(BF16), wglu_ref[...])
    out = gl[:, :S5_WIDTH] * jax.nn.sigmoid(gl[:, S5_WIDTH:])
    out_ref[...] = pltpu.einshape("tbc->btc", out.reshape(steps, nb, S5_WIDTH))


def _s5(u, lr, li, wb, wc, d, wglu, *, steps):
    batch, s, _ = u.shape
    rows = steps * batch
    const = lambda a: pl.BlockSpec(a.shape, lambda i: (0,) * a.ndim)
    blk = pl.BlockSpec((batch, steps, S5_WIDTH), lambda i: (0, i, 0))
    return pl.pallas_call(
        functools.partial(_s5_kernel, steps=steps),
        grid=(s // steps,),
        in_specs=[blk, const(lr), const(li), const(wb), const(wc), const(d), const(wglu)],
        out_specs=blk,
        out_shape=jax.ShapeDtypeStruct((batch, s, S5_WIDTH), F32),
        scratch_shapes=[pltpu.VMEM((rows, 2 * S5_CHUNKS * S5_CHUNK_STATE), F32),
                        pltpu.VMEM((batch, 2 * S5_CHUNKS * S5_CHUNK_STATE), F32)],
        compiler_params=_cparams(1), name="s5_mixer",
    )(u, lr, li, wb, wc, d, wglu)


def _merge_kernel(x_ref, g_ref, up_ref, o0_ref, l0_ref, o1_ref, l1_ref, o2_ref, l2_ref, ysb_ref, ys5_ref,
                  pw_ref, ps_ref, wg_ref, wb_ref, wo_ref, out_ref, ext_ref, lvl_ref, no1_ref, nl1_ref, no2_ref,
                  nl2_ref, *, tm):
    ti = pl.program_id(1)
    x = x_ref[...]
    xn = _rms(x, g_ref[...]).astype(BF16)

    @pl.when(ti == 0)
    def _():
        ext_ref[:POOL_HALO, :] = jnp.zeros((POOL_HALO, POOL_WIDTH), F32)
        ext_ref[pl.ds(tm + POOL_HALO, POOL_HALO), :] = jnp.zeros((POOL_HALO, POOL_WIDTH), F32)
        lvl_ref[:, pl.ds(tm + POOL_HALO, POOL_HALO), :] = jnp.zeros((2, POOL_HALO, POOL_GROUP), F32)

    ext_ref[pl.ds(POOL_HALO, tm), :] = up_ref[...]
    pos = ti * tm + lax.broadcasted_iota(jnp.int32, (tm, 1), 0)
    ys = []
    for gi, w in enumerate(POOL_WINDOWS):
        c0, c1 = gi * POOL_GROUP, (gi + 1) * POOL_GROUP
        src, have = ext_ref, 1
        while have < w:
            cols = slice(c0, c1) if src is ext_ref else slice(0, POOL_GROUP)
            nxt = src[pl.ds(have, tm + POOL_HALO), cols] + src[pl.ds(0, tm + POOL_HALO), cols]
            dst = lvl_ref.at[have.bit_length() % 2]
            dst[pl.ds(0, tm + POOL_HALO), :] = nxt
            src, have = dst, 2 * have
        off = have - 1
        cols = slice(c0, c1) if src is ext_ref else slice(0, POOL_GROUP)
        acc = src[pl.ds(POOL_HALO - off, tm), cols]
        cnt = jnp.minimum(pos + 1, w).astype(F32)
        p = acc / cnt - ext_ref[pl.ds(POOL_HALO, tm), c0:c1]
        ys.append(_dot(p.astype(BF16), pw_ref[gi]))
    y_pool = jnp.concatenate(ys, axis=1) * ps_ref[...]
    ext_ref[:POOL_HALO, :] = ext_ref[pl.ds(tm, POOL_HALO), :]

    nslab = DIL_GROUP_WIDTH // LANES

    def to_token_order(src_ref, dst_ref):
        dil = src_ref.shape[0]
        for r in range(dil):
            for sl in range(nslab):
                dst_ref[sl, pl.ds(r, tm // dil, stride=dil), :] = src_ref[r, :, sl * LANES:(sl + 1) * LANES]
        return jnp.concatenate([dst_ref[sl] for sl in range(nslab)], axis=1)

    o1, l1 = to_token_order(o1_ref, no1_ref), to_token_order(l1_ref, nl1_ref)
    o2, l2 = to_token_order(o2_ref, no2_ref), to_token_order(l2_ref, nl2_ref)
    l0 = l0_ref[...]
    m = jnp.maximum(jnp.maximum(l0, l1), l2)
    e0, e1, e2 = jnp.exp(l0 - m), jnp.exp(l1 - m), jnp.exp(l2 - m)
    y_dil = (e0 * o0_ref[...] + e1 * o1 + e2 * o2) / (e0 + e1 + e2)

    branches = (y_pool.astype(BF16), y_dil.astype(BF16), ysb_ref[...], ys5_ref[...].astype(BF16))
    merged = jnp.zeros((tm, x.shape[1]), F32)
    for bi, yb in enumerate(branches):
        gate = jax.nn.sigmoid(_dot(xn, wg_ref[bi]))
        merged = merged + gate * _dot(yb, wb_ref[BRANCH_ROWS[bi]:BRANCH_ROWS[bi + 1], :])
    out_ref[...] = x + _dot(merged.astype(BF16), wo_ref[...])


def _merge(x, g, up, o0, l0, o1, l1, o2, l2, ysb, ys5, pw, ps, wg, wb, wo, *, tm):
    b, s, d = x.shape
    dil1, dil2 = o1.shape[1], o2.shape[1]
    gw = DIL_GROUP_WIDTH
    tok = lambda w: pl.BlockSpec((None, tm, w), lambda bi, ti: (bi, ti, 0))
    res = lambda dil: pl.BlockSpec((None, dil, tm // dil, gw), lambda bi, ti: (bi, 0, ti, 0))
    const = lambda a: pl.BlockSpec(a.shape, lambda bi, ti: (0,) * a.ndim)
    in_specs = [tok(d), const(g), tok(POOL_WIDTH), tok(gw), tok(gw), res(dil1), res(dil1), res(dil2), res(dil2),
                tok(SB_WIDTH), tok(S5_WIDTH),
                const(pw), const(ps), const(wg), const(wb), const(wo)]
    return pl.pallas_call(
        functools.partial(_merge_kernel, tm=tm),
        grid=(b, s // tm), in_specs=in_specs, out_specs=tok(d),
        out_shape=jax.ShapeDtypeStruct((b, s, d), F32),
        scratch_shapes=[pltpu.VMEM((tm + 2 * POOL_HALO, POOL_WIDTH), F32),
                        pltpu.VMEM((2, tm + 2 * POOL_HALO, POOL_GROUP), F32)]
        + [pltpu.VMEM((gw // LANES, tm, LANES), F32)] * 4,
        compiler_params=_cparams(2), name="gated_merge",
    )(x, g, up, o0, l0, o1, l1, o2, l2, ysb, ys5, pw, ps, wg, wb, wo)


def _ffn_kernel(x_ref, g_ref, wup_ref, wdn_ref, fg_ref, out_ref, *, hidden, final):
    x = x_ref[...]
    hn = _rms(x, g_ref[...]).astype(BF16)
    gu = _dot(hn, wup_ref[...])
    h = (jax.nn.silu(gu[:, :hidden]) * gu[:, hidden:]).astype(BF16)
    y = x + _dot(h, wdn_ref[...])
    if final:
        y = _rms(y, fg_ref[...])
    out_ref[...] = y


def _ffn(x2d, g, wup, wdn, fg, *, tm, final):
    n, d = x2d.shape
    hidden = wdn.shape[0]
    const = lambda a: pl.BlockSpec(a.shape, lambda i: (0,) * a.ndim, pipeline_mode=pl.Buffered(1))
    return pl.pallas_call(
        functools.partial(_ffn_kernel, hidden=hidden, final=final),
        grid=(n // tm,),
        in_specs=[pl.BlockSpec((tm, d), lambda i: (i, 0)), const(g), const(wup), const(wdn), const(fg)],
        out_specs=pl.BlockSpec((tm, d), lambda i: (i, 0)),
        out_shape=jax.ShapeDtypeStruct((n, d), F32),
        compiler_params=_cparams(1), name="swiglu_final" if final else "swiglu",
    )(x2d, g, wup, wdn, fg)


def _t5_bucket(dist):
    exact = REL_BUCKETS // 2
    df = jnp.maximum(dist, 1).astype(F32)
    large = exact + (jnp.log(df / exact) / math.log(REL_MAX_DIST / exact)
                     * (REL_BUCKETS - exact)).astype(jnp.int32)
    large = jnp.minimum(large, REL_BUCKETS - 1)
    return jnp.where(dist < exact, dist, large)


def _dil_bias(rel_bias_g, dil):
    band = DIL_BAND
    i = jnp.arange(band)[:, None]
    c = jnp.arange(2 * band)[None, :]
    dist_sub = band + i - c
    in_band = (dist_sub >= 0) & (dist_sub <= band)
    buckets = _t5_bucket(jnp.clip(dist_sub, 0, band) * dil)
    onehot = (buckets[None] == jnp.arange(REL_BUCKETS)[:, None, None]).astype(F32)
    bias = jnp.einsum('kh,kic->hic', rel_bias_g.astype(F32), onehot, precision=lax.Precision.HIGHEST)
    bias = jnp.where(in_band[None], bias, NEG_INF)
    first = jnp.where((c >= band)[None], bias, NEG_INF)
    return jnp.stack([bias, first])


def _dil_weights(w_dil, group):
    hw = DIL_HEADS * DIL_HEAD_DIM
    cols = [w_dil[:, i * hw + group * DIL_GROUP_WIDTH:i * hw + (group + 1) * DIL_GROUP_WIDTH] for i in range(3)]
    return jnp.concatenate(cols, axis=1).astype(BF16)


def _s5_params(a_re, a_im, log_dt, b_re, b_im, c_re, c_im):
    lam = lax.complex(a_re.astype(F32), a_im.astype(F32))
    dt = jnp.exp(log_dt.astype(F32))[:, None]
    lam_bar = jnp.exp(lam * dt)
    b_bar = ((lam_bar - 1.0) / lam)[:, :, None] * lax.complex(b_re.astype(F32), b_im.astype(F32))
    gpc = S5_GROUPS // S5_CHUNKS
    eye = jnp.eye(gpc, dtype=F32)

    def pack_b(bm):
        bm = bm.reshape(S5_CHUNKS, gpc, S5_STATE, S5_CH)
        return jnp.einsum('cgpk,gh->cgkhp', bm, eye).reshape(S5_CHUNKS, S5_CHUNK_CH, S5_CHUNK_STATE)

    def pack_c(cm):
        cm = cm.reshape(S5_CHUNKS, gpc, S5_CH, S5_STATE)
        return jnp.einsum('cgkp,gh->cgphk', cm, eye).reshape(S5_CHUNKS, S5_CHUNK_STATE, S5_CHUNK_CH)

    wb = jnp.concatenate([pack_b(jnp.real(b_bar)), pack_b(jnp.imag(b_bar))], axis=2).astype(BF16)
    wc = jnp.concatenate([pack_c(c_re.astype(F32)), pack_c(-c_im.astype(F32))], axis=1).astype(BF16)
    lr = jnp.real(lam_bar).reshape(S5_CHUNKS, 1, S5_CHUNK_STATE)
    li = jnp.imag(lam_bar).reshape(S5_CHUNKS, 1, S5_CHUNK_STATE)
    return lr, li, wb, wc


def kernel(x, attn_norm_g, w_in, pool_w, pool_scale, rel_bias, s5_a_re, s5_a_im, s5_log_dt, s5_b_re, s5_b_im,
           s5_c_re, s5_c_im, s5_d, s5_w_glu, w_branch, w_gate, w_out, ffn_norm_g, w_up, w_down, final_norm_g):
    b, s, d = x.shape
    depth = w_in.shape[0]
    o1 = POOL_WIDTH
    o2 = o1 + 3 * DIL_HEADS * DIL_HEAD_DIM
    o3 = o2 + 3 * SB_WIDTH

    t = SB_TILE
    ntri = -(jnp.arange(t)[None, :] >= jnp.arange(t)[:, None]).astype(BF16)
    biases = [_dil_bias(rel_bias[:, g * DIL_HEADS_PER_GROUP:(g + 1) * DIL_HEADS_PER_GROUP], dil)
              for g, (_, dil) in enumerate(DIL_PAIRS)]
    key_head = jnp.arange(DIL_HEADS_PER_GROUP * 2 * DIL_BAND) // (2 * DIL_BAND)
    ones_bd = (key_head[:, None] == (jnp.arange(DIL_GROUP_WIDTH) // DIL_HEAD_DIM)[None, :]).astype(BF16)
    fg = final_norm_g.reshape(1, d).astype(F32)

    for l in range(depth):
        wl = w_in[l]
        g_attn = attn_norm_g[l].reshape(1, d).astype(F32)
        w_dil = wl[:, o1:o2]
        w_sb = wl[:, o2:o3]
        up, d0, d1, d2, q, k, vt, u_s5 = _in_proj(
            x, g_attn, wl[:, :o1].astype(BF16),
            _dil_weights(w_dil, 0), _dil_weights(w_dil, 1), _dil_weights(w_dil, 2),
            w_sb[:, :SB_WIDTH].astype(BF16), w_sb[:, SB_WIDTH:2 * SB_WIDTH].astype(BF16),
            w_sb[:, 2 * SB_WIDTH:].T.astype(BF16), wl[:, o3:].astype(BF16), tm=IN_PROJ_ROWS)

        oa0, la0 = _dil_attn(d0.reshape(b, 1, s, d0.shape[-1]), biases[0], ones_bd)
        oa1, la1 = _dil_attn(d1, biases[1], ones_bd)
        oa2, la2 = _dil_attn(d2, biases[2], ones_bd)
        y_sb = _sb_attn(q, k, vt, ntri)

        lr, li, wb, wc = _s5_params(s5_a_re[l], s5_a_im[l], s5_log_dt[l], s5_b_re[l], s5_b_im[l],
                                    s5_c_re[l], s5_c_im[l])
        y_s5 = _s5(u_s5, lr, li, wb, wc, s5_d[l].reshape(1, S5_WIDTH).astype(F32), s5_w_glu[l].astype(BF16),
                   steps=S5_STEPS)

        x = _merge(x, g_attn, up, oa0.reshape(b, s, -1), la0.reshape(b, s, -1), oa1, la1, oa2, la2,
                   y_sb, y_s5,
                   pool_w[l].astype(BF16), pool_scale[l].reshape(1, POOL_WIDTH).astype(F32),
                   w_gate[l].astype(BF16), w_branch[l].astype(BF16), w_out[l].astype(BF16), tm=MERGE_ROWS)

        x = _ffn(x.reshape(b * s, d), ffn_norm_g[l].reshape(1, d).astype(F32), w_up[l].astype(BF16),
                 w_down[l].astype(BF16), fg, tm=FFN_ROWS, final=(l == depth - 1)).reshape(b, s, d)
    return x
```

```python
import functools
import math

import jax
import jax.numpy as jnp
from jax import lax
from jax.experimental import pallas as pl
from jax.experimental.pallas import tpu as pltpu

F32 = jnp.float32
BF16 = jnp.bfloat16

EPS = 1e-6
POOL_WINDOWS = (2, 4, 8, 16)
POOL_GROUP = 128
POOL_WIDTH = 512
POOL_HALO = 16
DIL_PAIRS = ((128, 1), (512, 4), (2048, 16))
DIL_BAND = 128
DIL_HEADS_PER_GROUP = 4
DIL_HEADS = 12
DIL_HEAD_DIM = 64
DIL_GROUP_WIDTH = DIL_HEADS_PER_GROUP * DIL_HEAD_DIM
DIL_STEP_ROWS = 1024
DIL_BLOCKS_PER_GROUP = 4
REL_BUCKETS = 32
REL_MAX_DIST = 2048
SB_HEADS = 4
SB_HEAD_DIM = 128
SB_WIDTH = 512
SB_TILE = 256
S5_WIDTH = 512
S5_CH = 16
S5_GROUPS = 32
S5_STATE = 64
S5_CHUNKS = 4
S5_CHUNK_CH = S5_WIDTH // S5_CHUNKS
S5_CHUNK_STATE = (S5_GROUPS // S5_CHUNKS) * S5_STATE
BRANCH_ROWS = (0, 512, 768, 1280, 1792)
NEG_INF = -1e30
SB_UNDERFLOW = 160.0
LANES = 128
LOG2E = math.log2(math.e)

IN_PROJ_ROWS = 1024
MERGE_ROWS = 256
FFN_ROWS = 512
S5_STEPS = 128

VMEM_LIMIT = 56 * 1024 * 1024


def _cparams(n_axes):
    return pltpu.CompilerParams(dimension_semantics=("arbitrary",) * n_axes,
                                vmem_limit_bytes=VMEM_LIMIT)


def _layer_spec(a, layer, **kw):
    tail = (0,) * (a.ndim - 1)
    return pl.BlockSpec((None,) + a.shape[1:], lambda *_: (layer,) + tail, **kw)


def _dot(a, b):
    return jnp.dot(a, b, preferred_element_type=F32)


def _dot_nt(a, b):
    return lax.dot_general(a, b, (((1,), (1,)), ((), ())), preferred_element_type=F32)


def _rms(x, g):
    ms = jnp.mean(x * x, axis=-1, keepdims=True)
    return x * lax.rsqrt(ms + EPS) * g


def _in_proj_kernel(x_ref, g_ref, wpool_ref, wd0_ref, wd1_ref, wd2_ref, wq_ref, wk_ref, wvt_ref,
                    ws5_ref, pool_ref, d0_ref, d1_ref, d2_ref, q_ref, k_ref, vt_ref, s5_ref,
                    scr_ref, *, tm, sb_scale):
    xn = _rms(x_ref[...], g_ref[...]).astype(BF16)
    pool_ref[...] = _dot(xn, wpool_ref[...])
    s5_ref[...] = _dot(xn, ws5_ref[...])

    def dil_proj(w_ref):
        q = _dot(xn, w_ref[:, :DIL_GROUP_WIDTH]) * (1.0 / math.sqrt(DIL_HEAD_DIM))
        kv = _dot(xn, w_ref[:, DIL_GROUP_WIDTH:])
        return q, kv

    q0, kv0 = dil_proj(wd0_ref)
    d0_ref[:, :DIL_GROUP_WIDTH] = q0.astype(BF16)
    d0_ref[:, DIL_GROUP_WIDTH:] = kv0.astype(BF16)
    for w_ref, out_ref, dil in ((wd1_ref, d1_ref, DIL_PAIRS[1][1]), (wd2_ref, d2_ref, DIL_PAIRS[2][1])):
        qg, kvg = dil_proj(w_ref)
        qkv = jnp.concatenate([qg, kvg], axis=1)
        for sl in range(qkv.shape[1] // LANES):
            scr_ref[sl] = qkv[:, sl * LANES:(sl + 1) * LANES]
        for r in range(dil):
            for sl in range(qkv.shape[1] // LANES):
                out_ref[r, :, sl * LANES:(sl + 1) * LANES] = (
                    scr_ref[sl, pl.ds(r, tm // dil, stride=dil), :].astype(BF16))

    q_ref[...] = (_dot(xn, wq_ref[...]) * sb_scale).astype(BF16)
    k_ref[...] = _dot(xn, wk_ref[...]).astype(BF16)
    vt = _dot_nt(wvt_ref[...], xn).astype(BF16)
    for jb in range(tm // SB_TILE):
        vt_ref[jb] = vt[:, jb * SB_TILE:(jb + 1) * SB_TILE]


def _in_proj(x, layer, g, wpool, wd0, wd1, wd2, wq, wk, wvt, ws5, *, tm):
    b, s, d = x.shape
    dil1, dil2 = DIL_PAIRS[1][1], DIL_PAIRS[2][1]
    gw = 3 * DIL_GROUP_WIDTH
    const = lambda a: _layer_spec(a, layer, pipeline_mode=pl.Buffered(1))
    out_shape = (
        jax.ShapeDtypeStruct((b, s, POOL_WIDTH), F32),
        jax.ShapeDtypeStruct((b, s, gw), BF16),
        jax.ShapeDtypeStruct((b, dil1, s // dil1, gw), BF16),
        jax.ShapeDtypeStruct((b, dil2, s // dil2, gw), BF16),
        jax.ShapeDtypeStruct((b, s, SB_WIDTH), BF16),
        jax.ShapeDtypeStruct((b, s, SB_WIDTH), BF16),
        jax.ShapeDtypeStruct((b, s // SB_TILE, SB_WIDTH, SB_TILE), BF16),
        jax.ShapeDtypeStruct((b, s, S5_WIDTH), F32),
    )
    out_specs = (
        pl.BlockSpec((None, tm, POOL_WIDTH), lambda bi, ti: (bi, ti, 0)),
        pl.BlockSpec((None, tm, gw), lambda bi, ti: (bi, ti, 0)),
        pl.BlockSpec((None, dil1, tm // dil1, gw), lambda bi, ti: (bi, 0, ti, 0)),
        pl.BlockSpec((None, dil2, tm // dil2, gw), lambda bi, ti: (bi, 0, ti, 0)),
        pl.BlockSpec((None, tm, SB_WIDTH), lambda bi, ti: (bi, ti, 0)),
        pl.BlockSpec((None, tm, SB_WIDTH), lambda bi, ti: (bi, ti, 0)),
        pl.BlockSpec((None, tm // SB_TILE, SB_WIDTH, SB_TILE), lambda bi, ti: (bi, ti, 0, 0)),
        pl.BlockSpec((None, tm, S5_WIDTH), lambda bi, ti: (bi, ti, 0)),
    )
    in_specs = [
        pl.BlockSpec((None, tm, d), lambda bi, ti: (bi, ti, 0)),
        const(g), const(wpool), const(wd0), const(wd1), const(wd2), const(wq), const(wk), const(wvt), const(ws5),
    ]
    sb_scale = LOG2E / math.sqrt(SB_HEAD_DIM)
    return pl.pallas_call(
        functools.partial(_in_proj_kernel, tm=tm, sb_scale=sb_scale),
        grid=(b, s // tm), in_specs=in_specs, out_specs=out_specs, out_shape=out_shape,
        scratch_shapes=[pltpu.VMEM((gw // LANES, tm, LANES), F32)],
        compiler_params=_cparams(2), name="in_proj",
    )(x, g, wpool, wd0, wd1, wd2, wq, wk, wvt, ws5)


def _dil_attn_kernel(cur_ref, halo_ref, bias_ref, ones_ref, o_ref, lse_ref):
    nres, rows, _ = cur_ref.shape
    nblk = rows // DIL_BAND
    gw = DIL_GROUP_WIDTH
    nh = DIL_HEADS_PER_GROUP
    step = pl.program_id(2)
    lane_head = lax.broadcasted_iota(jnp.int32, (1, gw), 1) // DIL_HEAD_DIM
    hms = [lane_head == h for h in range(nh)]
    hmuls = [hm.astype(BF16) for hm in hms]
    ones_bd = ones_ref[...]

    def blocks(items):
        r0s = [n * DIL_BAND if isinstance(n, int) else pl.multiple_of(n * DIL_BAND, DIL_BAND) for _, n in items]
        kvs = []
        for (res, n), r0 in zip(items, r0s):
            if isinstance(n, int) and n == 0:
                kvs.append(jnp.concatenate([halo_ref[res, :, gw:], cur_ref[res, :DIL_BAND, gw:]], axis=0))
            else:
                kvs.append(cur_ref[res, pl.ds(r0 - DIL_BAND, 2 * DIL_BAND), gw:])
        ss = []
        for (res, _), r0, kvb in zip(items, r0s, kvs):
            q = cur_ref[res, pl.ds(r0, DIL_BAND), :gw]
            qs = jnp.concatenate([q * hmul for hmul in hmuls], axis=0)
            ss.append(_dot_nt(qs, kvb[:, :gw]))
        ps, ms = [], []
        for (_, n), s in zip(items, ss):
            if isinstance(n, int) and n > 0:
                first = 0
            else:
                first = (step * nblk + n == 0).astype(jnp.int32)
            p_h, m_h = [], []
            for h in range(nh):
                sh = s[h * DIL_BAND:(h + 1) * DIL_BAND, :] + bias_ref[first, h]
                m = jnp.max(sh, axis=-1, keepdims=True)
                p_h.append(jnp.exp(sh - m).astype(BF16))
                m_h.append(m)
            ps.append(jnp.concatenate(p_h, axis=1))
            ms.append(m_h)
        accs, dens = [], []
        for p, kvb in zip(ps, kvs):
            v = kvb[:, gw:]
            v_bd = jnp.concatenate([v * hmul for hmul in hmuls], axis=0)
            accs.append(_dot(p, v_bd))
            dens.append(_dot(p, ones_bd))
        for (res, _), r0, acc, den, m_h in zip(items, r0s, accs, dens, ms):
            m_b = m_h[0]
            for h in range(1, nh):
                m_b = jnp.where(hms[h], m_h[h], m_b)
            o_ref[res, pl.ds(r0, DIL_BAND), :] = acc / den
            lse_ref[res, pl.ds(r0, DIL_BAND), :] = m_b + jnp.log(den)

    group = DIL_BLOCKS_PER_GROUP
    if nres > 1:
        items = [(res, n) for res in range(nres) for n in range(nblk)]
        for g0 in range(0, len(items), group):
            blocks(items[g0:g0 + group])
    else:
        blocks([(0, n) for n in range(group)])

        def body(g, carry):
            blocks([(0, g * group + u) for u in range(group)])
            return carry

        lax.fori_loop(1, nblk // group, body, 0)


def _dil_attn(qkv, bias, ones_bd):
    b, dil, length, gw3 = qkv.shape
    rows = min(length, DIL_STEP_ROWS)
    nres = DIL_STEP_ROWS // rows
    rb = rows // DIL_BAND
    out_shape = (jax.ShapeDtypeStruct((b, dil, length, DIL_GROUP_WIDTH), F32),) * 2
    blk = lambda w: pl.BlockSpec((None, nres, rows, w), lambda bi, ri, i: (bi, ri, i, 0))
    halo = pl.BlockSpec((None, nres, DIL_BAND, gw3),
                        lambda bi, ri, i: (bi, ri, jnp.maximum(i * rb - 1, 0), 0))
    return pl.pallas_call(
        _dil_attn_kernel,
        grid=(b, dil // nres, length // rows),
        in_specs=[blk(gw3), halo, pl.BlockSpec(bias.shape, lambda bi, ri, i: (0, 0, 0, 0)),
                  pl.BlockSpec(ones_bd.shape, lambda bi, ri, i: (0, 0))],
        out_specs=(blk(DIL_GROUP_WIDTH), blk(DIL_GROUP_WIDTH)), out_shape=out_shape,
        compiler_params=_cparams(3), name=f"dil_attn_d{dil}",
    )(qkv, qkv, bias, ones_bd)


def _sb_attn_kernel(q_ref, k_ref, vt_ref, ntri_ref, o_ref, acc_ref):
    tk = tq = SB_TILE
    i = pl.program_id(1)
    ntri = ntri_ref[...]
    row = lax.broadcasted_iota(jnp.int32, (tk, tq), 0)
    col = lax.broadcasted_iota(jnp.int32, (tk, tq), 1)
    sign = jnp.uint32(0x80000000)
    heads = range(SB_HEADS)
    hsl = [slice(h * SB_HEAD_DIM, (h + 1) * SB_HEAD_DIM) for h in heads]

    def tiles(specs, runs, first):
        chains = [(j, mask, h) for j, mask in specs for h in heads]
        kq = lambda j, h: _dot_nt(k_ref[pl.ds(pl.multiple_of(j * tk, tk), tk), hsl[h]], q_ref[:, hsl[h]])
        zs = [kq(j, h) for j, _, h in chains]
        his, z0s = [], []
        for (_, mask, _), z in zip(chains, zs):
            neg_abs = lax.bitcast_convert_type(lax.bitcast_convert_type(z, jnp.uint32) | sign, F32)
            sp = jnp.maximum(z, 0.0) + jnp.log(1.0 + jnp.exp2(neg_abs)) * LOG2E
            if mask is not None:
                sp = jnp.where(mask, sp, 0.0)
            his.append(sp.astype(BF16))
            z0s.append(z[0:1, :])
        ds = [z + _dot(ntri, hi) for z, hi in zip(zs, his)]
        runs = list(runs)
        ws = []
        for (_, mask, h), z0, d in zip(chains, z0s, ds):
            w = jnp.exp2(d - runs[h])
            if mask is not None:
                w = jnp.where(mask, w, 0.0)
            ws.append(w.astype(BF16))
            runs[h] = runs[h] + (z0 - d[0:1, :])
        pvs = [_dot(vt_ref[j, hsl[h], :], w) for (j, _, h), w in zip(chains, ws)]
        for h in heads:
            total = pvs[h]
            for t in range(1, len(specs)):
                total = total + pvs[t * SB_HEADS + h]
            if first:
                acc_ref[h] = total
            else:
                acc_ref[h] += total
        return tuple(runs)

    zero_runs = (jnp.zeros((1, tq), F32),) * SB_HEADS
    causal = row < col

    def min_run(runs):
        return jnp.min(jnp.minimum(jnp.minimum(runs[0], runs[1]), jnp.minimum(runs[2], runs[3])))

    @pl.when(i == 0)
    def _():
        tiles([(0, causal)], zero_runs, True)

    @pl.when(i > 0)
    def _():
        runs = tiles([(i, causal), (i - 1, None)], zero_runs, True)
        nfull = i - 1

        def cond(carry):
            jj, mr, _ = carry
            return jnp.logical_and(jj < nfull, mr < SB_UNDERFLOW)

        def body(carry):
            jj, _, runs = carry
            runs = tiles([(nfull - 1 - jj, None)], runs, False)
            return jj + 1, min_run(runs), runs

        lax.while_loop(cond, body, (jnp.int32(0), min_run(runs), runs))

    for h in heads:
        o_ref[:, hsl[h]] = acc_ref[h].T.astype(o_ref.dtype)


def _sb_attn(q, k, vt, ntri):
    b, s, _ = q.shape
    tq = tk = SB_TILE
    return pl.pallas_call(
        _sb_attn_kernel,
        grid=(b, s // tq),
        in_specs=[
            pl.BlockSpec((None, tq, SB_WIDTH), lambda bi, i: (bi, i, 0)),
            pl.BlockSpec((None, s, SB_WIDTH), lambda bi, i: (bi, 0, 0)),
            pl.BlockSpec((None, s // tk, SB_WIDTH, tk), lambda bi, i: (bi, 0, 0, 0)),
            pl.BlockSpec((tk, tk), lambda bi, i: (0, 0)),
        ],
        out_specs=pl.BlockSpec((None, tq, SB_WIDTH), lambda bi, i: (bi, i, 0)),
        out_shape=jax.ShapeDtypeStruct((b, s, SB_WIDTH), BF16),
        scratch_shapes=[pltpu.VMEM((SB_HEADS, SB_HEAD_DIM, tq), F32)],
        compiler_params=_cparams(2), name="sb_attn",
    )(q, k, vt, ntri)


def _gelu_tanh(x):
    return 0.5 * x * (1.0 + jnp.tanh(math.sqrt(2.0 / math.pi) * (x + 0.044715 * (x * x * x))))


def _s5_kernel(u_ref, lr_ref, li_ref, wb_ref, wc_ref, d_ref, wglu_ref, out_ref, bu_ref, h_ref, *, steps):
    cs = S5_CHUNK_STATE

    @pl.when(pl.program_id(0) == 0)
    def _():
        h_ref[...] = jnp.zeros_like(h_ref)

    nb = h_ref.shape[0]
    u = pltpu.einshape("btc->tbc", u_ref[...]).reshape(steps * nb, S5_WIDTH)
    ub = u.astype(BF16)
    for c in range(S5_CHUNKS):
        bu_ref[:, 2 * cs * c:2 * cs * (c + 1)] = _dot(ub[:, S5_CHUNK_CH * c:S5_CHUNK_CH * (c + 1)], wb_ref[c])

    for c0 in range(0, S5_CHUNKS, 2):
        lam = [(jnp.broadcast_to(lr_ref[c], (nb, cs)), jnp.broadcast_to(li_ref[c], (nb, cs)))
               for c in (c0, c0 + 1)]
        offs = [2 * cs * c for c in (c0, c0 + 1)]

        def step(t, carry):
            r = pl.multiple_of(t * nb, nb)
            new = []
            for (lr, li), off, (hr, hi) in zip(lam, offs, carry):
                br = bu_ref[pl.ds(r, nb), off:off + cs]
                bi = bu_ref[pl.ds(r, nb), off + cs:off + 2 * cs]
                nhr = lr * hr - li * hi + br
                nhi = lr * hi + li * hr + bi
                bu_ref[pl.ds(r, nb), off:off + cs] = nhr
                bu_ref[pl.ds(r, nb), off + cs:off + 2 * cs] = nhi
                new.append((nhr, nhi))
            return tuple(new)

        init = tuple((h_ref[:, off:off + cs], h_ref[:, off + cs:off + 2 * cs]) for off in offs)
        fin = lax.fori_loop(0, steps, step, init, unroll=2)
        for off, (hr, hi) in zip(offs, fin):
            h_ref[:, off:off + cs] = hr
            h_ref[:, off + cs:off + 2 * cs] = hi

    y = jnp.concatenate(
        [_dot(bu_ref[:, 2 * cs * c:2 * cs * (c + 1)].astype(BF16), wc_ref[c]) for c in range(S5_CHUNKS)],
        axis=1)
    y = _gelu_tanh(y + d_ref[...] * u)
    gl = _dot(y.astype(BF16), wglu_ref[...])
    out = gl[:, :S5_WIDTH] * jax.nn.sigmoid(gl[:, S5_WIDTH:])
    out_ref[...] = pltpu.einshape("tbc->btc", out.reshape(steps, nb, S5_WIDTH))


def _s5(u, layer, lr, li, wb, wc, d, wglu, *, steps):
    batch, s, _ = u.shape
    rows = steps * batch
    const = lambda a: _layer_spec(a, layer)
    blk = pl.BlockSpec((batch, steps, S5_WIDTH), lambda i: (0, i, 0))
    return pl.pallas_call(
        functools.partial(_s5_kernel, steps=steps),
        grid=(s // steps,),
        in_specs=[blk, const(lr), const(li), const(wb), const(wc), const(d), const(wglu)],
        out_specs=blk,
        out_shape=jax.ShapeDtypeStruct((batch, s, S5_WIDTH), F32),
        scratch_shapes=[pltpu.VMEM((rows, 2 * S5_CHUNKS * S5_CHUNK_STATE), F32),
                        pltpu.VMEM((batch, 2 * S5_CHUNKS * S5_CHUNK_STATE), F32)],
        compiler_params=_cparams(1), name="s5_mixer",
    )(u, lr, li, wb, wc, d, wglu)


def _merge_kernel(x_ref, g_ref, up_ref, o0_ref, l0_ref, o1_ref, l1_ref, o2_ref, l2_ref, ysb_ref, ys5_ref,
                  pw_ref, ps_ref, wg_ref, wb_ref, wo_ref, out_ref, ext_ref, lvl_ref, no1_ref, nl1_ref, no2_ref,
                  nl2_ref, *, tm):
    ti = pl.program_id(1)
    x = x_ref[...]
    xn = _rms(x, g_ref[...]).astype(BF16)

    @pl.when(ti == 0)
    def _():
        ext_ref[:POOL_HALO, :] = jnp.zeros((POOL_HALO, POOL_WIDTH), F32)
        ext_ref[pl.ds(tm + POOL_HALO, POOL_HALO), :] = jnp.zeros((POOL_HALO, POOL_WIDTH), F32)
        lvl_ref[:, pl.ds(tm + POOL_HALO, POOL_HALO), :] = jnp.zeros((2, POOL_HALO, POOL_GROUP), F32)

    ext_ref[pl.ds(POOL_HALO, tm), :] = up_ref[...]
    pos = ti * tm + lax.broadcasted_iota(jnp.int32, (tm, 1), 0)
    ys = []
    for gi, w in enumerate(POOL_WINDOWS):
        c0, c1 = gi * POOL_GROUP, (gi + 1) * POOL_GROUP
        src, have = ext_ref, 1
        while have < w:
            cols = slice(c0, c1) if src is ext_ref else slice(0, POOL_GROUP)
            nxt = src[pl.ds(have, tm + POOL_HALO), cols] + src[pl.ds(0, tm + POOL_HALO), cols]
            dst = lvl_ref.at[have.bit_length() % 2]
            dst[pl.ds(0, tm + POOL_HALO), :] = nxt
            src, have = dst, 2 * have
        off = have - 1
        cols = slice(c0, c1) if src is ext_ref else slice(0, POOL_GROUP)
        acc = src[pl.ds(POOL_HALO - off, tm), cols]
        cnt = jnp.minimum(pos + 1, w).astype(F32)
        p = acc / cnt - ext_ref[pl.ds(POOL_HALO, tm), c0:c1]
        ys.append(_dot(p.astype(BF16), pw_ref[gi]))
    y_pool = jnp.concatenate(ys, axis=1) * ps_ref[...]
    ext_ref[:POOL_HALO, :] = ext_ref[pl.ds(tm, POOL_HALO), :]

    nslab = DIL_GROUP_WIDTH // LANES

    def to_token_order(src_ref, dst_ref):
        dil = src_ref.shape[0]
        for r in range(dil):
            for sl in range(nslab):
                dst_ref[sl, pl.ds(r, tm // dil, stride=dil), :] = src_ref[r, :, sl * LANES:(sl + 1) * LANES]
        return jnp.concatenate([dst_ref[sl] for sl in range(nslab)], axis=1)

    o1, l1 = to_token_order(o1_ref, no1_ref), to_token_order(l1_ref, nl1_ref)
    o2, l2 = to_token_order(o2_ref, no2_ref), to_token_order(l2_ref, nl2_ref)
    l0 = l0_ref[...]
    m = jnp.maximum(jnp.maximum(l0, l1), l2)
    e0, e1, e2 = jnp.exp(l0 - m), jnp.exp(l1 - m), jnp.exp(l2 - m)
    y_dil = (e0 * o0_ref[...] + e1 * o1 + e2 * o2) / (e0 + e1 + e2)

    branches = (y_pool.astype(BF16), y_dil.astype(BF16), ysb_ref[...], ys5_ref[...].astype(BF16))
    merged = jnp.zeros((tm, x.shape[1]), F32)
    for bi, yb in enumerate(branches):
        gate = jax.nn.sigmoid(_dot(xn, wg_ref[bi]))
        merged = merged + gate * _dot(yb, wb_ref[BRANCH_ROWS[bi]:BRANCH_ROWS[bi + 1], :])
    out_ref[...] = x + _dot(merged.astype(BF16), wo_ref[...])


def _merge(x, layer, g, up, o0, l0, o1, l1, o2, l2, ysb, ys5, pw, ps, wg, wb, wo, *, tm):
    b, s, d = x.shape
    dil1, dil2 = o1.shape[1], o2.shape[1]
    gw = DIL_GROUP_WIDTH
    tok = lambda w: pl.BlockSpec((None, tm, w), lambda bi, ti: (bi, ti, 0))
    res = lambda dil: pl.BlockSpec((None, dil, tm // dil, gw), lambda bi, ti: (bi, 0, ti, 0))
    const = lambda a: _layer_spec(a, layer)
    in_specs = [tok(d), const(g), tok(POOL_WIDTH), tok(gw), tok(gw), res(dil1), res(dil1), res(dil2), res(dil2),
                tok(SB_WIDTH), tok(S5_WIDTH),
                const(pw), const(ps), const(wg), const(wb), const(wo)]
    return pl.pallas_call(
        functools.partial(_merge_kernel, tm=tm),
        grid=(b, s // tm), in_specs=in_specs, out_specs=tok(d),
        out_shape=jax.ShapeDtypeStruct((b, s, d), F32),
        scratch_shapes=[pltpu.VMEM((tm + 2 * POOL_HALO, POOL_WIDTH), F32),
                        pltpu.VMEM((2, tm + 2 * POOL_HALO, POOL_GROUP), F32)]
        + [pltpu.VMEM((gw // LANES, tm, LANES), F32)] * 4,
        compiler_params=_cparams(2), name="gated_merge",
    )(x, g, up, o0, l0, o1, l1, o2, l2, ysb, ys5, pw, ps, wg, wb, wo)


def _ffn_kernel(x_ref, g_ref, wup_ref, wdn_ref, fg_ref, out_ref, *, hidden, final):
    x = x_ref[...]
    hn = _rms(x, g_ref[...]).astype(BF16)
    gu = _dot(hn, wup_ref[...])
    h = (jax.nn.silu(gu[:, :hidden]) * gu[:, hidden:]).astype(BF16)
    y = x + _dot(h, wdn_ref[...])
    if final:
        y = _rms(y, fg_ref[...])
    out_ref[...] = y


def _ffn(x2d, layer, g, wup, wdn, fg, *, tm, final):
    n, d = x2d.shape
    hidden = wdn.shape[-2]
    const = lambda a: _layer_spec(a, layer, pipeline_mode=pl.Buffered(1))
    return pl.pallas_call(
        functools.partial(_ffn_kernel, hidden=hidden, final=final),
        grid=(n // tm,),
        in_specs=[pl.BlockSpec((tm, d), lambda i: (i, 0)), const(g), const(wup), const(wdn),
                  pl.BlockSpec(fg.shape, lambda i: (0, 0))],
        out_specs=pl.BlockSpec((tm, d), lambda i: (i, 0)),
        out_shape=jax.ShapeDtypeStruct((n, d), F32),
        compiler_params=_cparams(1), name="swiglu_final" if final else "swiglu",
    )(x2d, g, wup, wdn, fg)


def _t5_bucket(dist):
    exact = REL_BUCKETS // 2
    df = jnp.maximum(dist, 1).astype(F32)
    large = exact + (jnp.log(df / exact) / math.log(REL_MAX_DIST / exact)
                     * (REL_BUCKETS - exact)).astype(jnp.int32)
    large = jnp.minimum(large, REL_BUCKETS - 1)
    return jnp.where(dist < exact, dist, large)


def _dil_bias(rel_bias_g, dil):
    band = DIL_BAND
    i = jnp.arange(band)[:, None]
    c = jnp.arange(2 * band)[None, :]
    dist_sub = band + i - c
    in_band = (dist_sub >= 0) & (dist_sub <= band)
    buckets = _t5_bucket(jnp.clip(dist_sub, 0, band) * dil)
    onehot = (buckets[None] == jnp.arange(REL_BUCKETS)[:, None, None]).astype(F32)
    bias = jnp.einsum('kh,kic->hic', rel_bias_g.astype(F32), onehot, precision=lax.Precision.HIGHEST)
    bias = jnp.where(in_band[None], bias, NEG_INF)
    first = jnp.where((c >= band)[None], bias, NEG_INF)
    return jnp.stack([bias, first])


def _dil_weights(w_dil, group):
    hw = DIL_HEADS * DIL_HEAD_DIM
    cols = [w_dil[..., i * hw + group * DIL_GROUP_WIDTH:i * hw + (group + 1) * DIL_GROUP_WIDTH] for i in range(3)]
    return jnp.concatenate(cols, axis=-1).astype(BF16)


def _s5_params(a_re, a_im, log_dt, b_re, b_im, c_re, c_im):
    lam = lax.complex(a_re.astype(F32), a_im.astype(F32))
    dt = jnp.exp(log_dt.astype(F32))[:, None]
    lam_bar = jnp.exp(lam * dt)
    b_bar = ((lam_bar - 1.0) / lam)[:, :, None] * lax.complex(b_re.astype(F32), b_im.astype(F32))
    gpc = S5_GROUPS // S5_CHUNKS
    eye = jnp.eye(gpc, dtype=F32)

    def pack_b(bm):
        bm = bm.reshape(S5_CHUNKS, gpc, S5_STATE, S5_CH)
        return jnp.einsum('cgpk,gh->cgkhp', bm, eye).reshape(S5_CHUNKS, S5_CHUNK_CH, S5_CHUNK_STATE)

    def pack_c(cm):
        cm = cm.reshape(S5_CHUNKS, gpc, S5_CH, S5_STATE)
        return jnp.einsum('cgkp,gh->cgphk', cm, eye).reshape(S5_CHUNKS, S5_CHUNK_STATE, S5_CHUNK_CH)

    wb = jnp.concatenate([pack_b(jnp.real(b_bar)), pack_b(jnp.imag(b_bar))], axis=2).astype(BF16)
    wc = jnp.concatenate([pack_c(c_re.astype(F32)), pack_c(-c_im.astype(F32))], axis=1).astype(BF16)
    lr = jnp.real(lam_bar).reshape(S5_CHUNKS, 1, S5_CHUNK_STATE)
    li = jnp.imag(lam_bar).reshape(S5_CHUNKS, 1, S5_CHUNK_STATE)
    return lr, li, wb, wc


def kernel(x, attn_norm_g, w_in, pool_w, pool_scale, rel_bias, s5_a_re, s5_a_im, s5_log_dt, s5_b_re, s5_b_im,
           s5_c_re, s5_c_im, s5_d, s5_w_glu, w_branch, w_gate, w_out, ffn_norm_g, w_up, w_down, final_norm_g):
    b, s, d = x.shape
    depth = w_in.shape[0]
    o1 = POOL_WIDTH
    o2 = o1 + 3 * DIL_HEADS * DIL_HEAD_DIM
    o3 = o2 + 3 * SB_WIDTH

    t = SB_TILE
    ntri = -(jnp.arange(t)[None, :] >= jnp.arange(t)[:, None]).astype(BF16)
    biases = [_dil_bias(rel_bias[:, g * DIL_HEADS_PER_GROUP:(g + 1) * DIL_HEADS_PER_GROUP], dil)
              for g, (_, dil) in enumerate(DIL_PAIRS)]
    key_head = jnp.arange(DIL_HEADS_PER_GROUP * 2 * DIL_BAND) // (2 * DIL_BAND)
    ones_bd = (key_head[:, None] == (jnp.arange(DIL_GROUP_WIDTH) // DIL_HEAD_DIM)[None, :]).astype(BF16)
    fg = final_norm_g.reshape(1, d).astype(F32)

    w_dil = w_in[:, :, o1:o2]
    w_sb = w_in[:, :, o2:o3]
    in_w = (attn_norm_g.reshape(depth, 1, d).astype(F32), w_in[:, :, :o1].astype(BF16),
            _dil_weights(w_dil, 0), _dil_weights(w_dil, 1), _dil_weights(w_dil, 2),
            w_sb[:, :, :SB_WIDTH].astype(BF16), w_sb[:, :, SB_WIDTH:2 * SB_WIDTH].astype(BF16),
            jnp.swapaxes(w_sb[:, :, 2 * SB_WIDTH:], 1, 2).astype(BF16), w_in[:, :, o3:].astype(BF16))
    s5_w = jax.vmap(_s5_params)(s5_a_re, s5_a_im, s5_log_dt, s5_b_re, s5_b_im, s5_c_re, s5_c_im) + (
        s5_d.reshape(depth, 1, S5_WIDTH).astype(F32), s5_w_glu.astype(BF16))
    merge_w = (pool_w.astype(BF16), pool_scale.reshape(depth, 1, POOL_WIDTH).astype(F32),
               w_gate.astype(BF16), w_branch.astype(BF16), w_out.astype(BF16))
    ffn_w = (ffn_norm_g.reshape(depth, 1, d).astype(F32), w_up.astype(BF16), w_down.astype(BF16))

    for l in range(depth):
        up, d0, d1, d2, q, k, vt, u_s5 = _in_proj(x, l, *in_w, tm=IN_PROJ_ROWS)

        oa0, la0 = _dil_attn(d0.reshape(b, 1, s, d0.shape[-1]), biases[0], ones_bd)
        oa1, la1 = _dil_attn(d1, biases[1], ones_bd)
        oa2, la2 = _dil_attn(d2, biases[2], ones_bd)
        y_sb = _sb_attn(q, k, vt, ntri)
        y_s5 = _s5(u_s5, l, *s5_w, steps=S5_STEPS)

        x = _merge(x, l, in_w[0], up, oa0.reshape(b, s, -1), la0.reshape(b, s, -1), oa1, la1, oa2, la2,
                   y_sb, y_s5, *merge_w, tm=MERGE_ROWS)
        x = _ffn(x.reshape(b * s, d), l, *ffn_w, fg, tm=FFN_ROWS, final=(l == depth - 1)).reshape(b, s, d)
    return x
```

```python
import functools
import math

import jax
import jax.numpy as jnp
from jax import lax
from jax.experimental import pallas as pl
from jax.experimental.pallas import tpu as pltpu

F32 = jnp.float32
BF16 = jnp.bfloat16

EPS = 1e-6
POOL_WINDOWS = (2, 4, 8, 16)
POOL_GROUP = 128
POOL_WIDTH = 512
POOL_HALO = 16
DIL_PAIRS = ((128, 1), (512, 4), (2048, 16))
DIL_BAND = 128
DIL_HEADS_PER_GROUP = 4
DIL_HEADS = 12
DIL_HEAD_DIM = 64
DIL_GROUP_WIDTH = DIL_HEADS_PER_GROUP * DIL_HEAD_DIM
DIL_STEP_ROWS = 1024
DIL_BLOCKS_PER_GROUP = 4
REL_BUCKETS = 32
REL_MAX_DIST = 2048
SB_HEADS = 4
SB_HEAD_DIM = 128
SB_WIDTH = 512
SB_TILE = 256
S5_WIDTH = 512
S5_CH = 16
S5_GROUPS = 32
S5_STATE = 64
S5_CHUNKS = 4
S5_CHUNK_CH = S5_WIDTH // S5_CHUNKS
S5_CHUNK_STATE = (S5_GROUPS // S5_CHUNKS) * S5_STATE
BRANCH_ROWS = (0, 512, 768, 1280, 1792)
NEG_INF = -1e30
SB_UNDERFLOW = 160.0
LANES = 128
LOG2E = math.log2(math.e)

IN_PROJ_ROWS = 1024
MERGE_ROWS = 256
FFN_ROWS = 512
S5_STEPS = 128

VMEM_LIMIT = 56 * 1024 * 1024


def _cparams(n_axes):
    return pltpu.CompilerParams(dimension_semantics=("arbitrary",) * n_axes,
                                vmem_limit_bytes=VMEM_LIMIT)


def _layer_spec(a, layer, **kw):
    tail = (0,) * (a.ndim - 1)
    return pl.BlockSpec((None,) + a.shape[1:], lambda *_: (layer,) + tail, **kw)


def _dot(a, b):
    return jnp.dot(a, b, preferred_element_type=F32)


def _dot_nt(a, b):
    return lax.dot_general(a, b, (((1,), (1,)), ((), ())), preferred_element_type=F32)


def _rms(x, g):
    ms = jnp.mean(x * x, axis=-1, keepdims=True)
    return x * lax.rsqrt(ms + EPS) * g


def _in_proj_kernel(x_ref, g_ref, wpool_ref, wd0_ref, wd1_ref, wd2_ref, wq_ref, wk_ref, wvt_ref,
                    ws5_ref, pool_ref, d0_ref, d1_ref, d2_ref, q_ref, k_ref, vt_ref, s5_ref,
                    scr_ref, *, tm, sb_scale):
    xn = _rms(x_ref[...], g_ref[...]).astype(BF16)
    pool_ref[...] = _dot(xn, wpool_ref[...])
    s5_ref[...] = _dot(xn, ws5_ref[...])

    def dil_proj(w_ref):
        q = _dot(xn, w_ref[:, :DIL_GROUP_WIDTH]) * (1.0 / math.sqrt(DIL_HEAD_DIM))
        kv = _dot(xn, w_ref[:, DIL_GROUP_WIDTH:])
        return q, kv

    q0, kv0 = dil_proj(wd0_ref)
    d0_ref[:, :DIL_GROUP_WIDTH] = q0.astype(BF16)
    d0_ref[:, DIL_GROUP_WIDTH:] = kv0.astype(BF16)
    for w_ref, out_ref, dil in ((wd1_ref, d1_ref, DIL_PAIRS[1][1]), (wd2_ref, d2_ref, DIL_PAIRS[2][1])):
        qg, kvg = dil_proj(w_ref)
        qkv = jnp.concatenate([qg, kvg], axis=1)
        for sl in range(qkv.shape[1] // LANES):
            scr_ref[sl] = qkv[:, sl * LANES:(sl + 1) * LANES]
        for r in range(dil):
            for sl in range(qkv.shape[1] // LANES):
                out_ref[r, :, sl * LANES:(sl + 1) * LANES] = (
                    scr_ref[sl, pl.ds(r, tm // dil, stride=dil), :].astype(BF16))

    q_ref[...] = (_dot(xn, wq_ref[...]) * sb_scale).astype(BF16)
    k_ref[...] = _dot(xn, wk_ref[...]).astype(BF16)
    vt = _dot_nt(wvt_ref[...], xn).astype(BF16)
    for jb in range(tm // SB_TILE):
        vt_ref[jb] = vt[:, jb * SB_TILE:(jb + 1) * SB_TILE]


def _in_proj(x, layer, g, wpool, wd0, wd1, wd2, wq, wk, wvt, ws5, *, tm):
    b, s, d = x.shape
    dil1, dil2 = DIL_PAIRS[1][1], DIL_PAIRS[2][1]
    gw = 3 * DIL_GROUP_WIDTH
    const = lambda a: _layer_spec(a, layer, pipeline_mode=pl.Buffered(1))
    out_shape = (
        jax.ShapeDtypeStruct((b, s, POOL_WIDTH), F32),
        jax.ShapeDtypeStruct((b, s, gw), BF16),
        jax.ShapeDtypeStruct((b, dil1, s // dil1, gw), BF16),
        jax.ShapeDtypeStruct((b, dil2, s // dil2, gw), BF16),
        jax.ShapeDtypeStruct((b, s, SB_WIDTH), BF16),
        jax.ShapeDtypeStruct((b, s, SB_WIDTH), BF16),
        jax.ShapeDtypeStruct((b, s // SB_TILE, SB_WIDTH, SB_TILE), BF16),
        jax.ShapeDtypeStruct((b, s, S5_WIDTH), F32),
    )
    out_specs = (
        pl.BlockSpec((None, tm, POOL_WIDTH), lambda bi, ti: (bi, ti, 0)),
        pl.BlockSpec((None, tm, gw), lambda bi, ti: (bi, ti, 0)),
        pl.BlockSpec((None, dil1, tm // dil1, gw), lambda bi, ti: (bi, 0, ti, 0)),
        pl.BlockSpec((None, dil2, tm // dil2, gw), lambda bi, ti: (bi, 0, ti, 0)),
        pl.BlockSpec((None, tm, SB_WIDTH), lambda bi, ti: (bi, ti, 0)),
        pl.BlockSpec((None, tm, SB_WIDTH), lambda bi, ti: (bi, ti, 0)),
        pl.BlockSpec((None, tm // SB_TILE, SB_WIDTH, SB_TILE), lambda bi, ti: (bi, ti, 0, 0)),
        pl.BlockSpec((None, tm, S5_WIDTH), lambda bi, ti: (bi, ti, 0)),
    )
    in_specs = [
        pl.BlockSpec((None, tm, d), lambda bi, ti: (bi, ti, 0)),
        const(g), const(wpool), const(wd0), const(wd1), const(wd2), const(wq), const(wk), const(wvt), const(ws5),
    ]
    sb_scale = LOG2E / math.sqrt(SB_HEAD_DIM)
    return pl.pallas_call(
        functools.partial(_in_proj_kernel, tm=tm, sb_scale=sb_scale),
        grid=(b, s // tm), in_specs=in_specs, out_specs=out_specs, out_shape=out_shape,
        scratch_shapes=[pltpu.VMEM((gw // LANES, tm, LANES), F32)],
        compiler_params=_cparams(2), name="in_proj",
    )(x, g, wpool, wd0, wd1, wd2, wq, wk, wvt, ws5)


def _dil_attn_kernel(cur_ref, halo_ref, bias_ref, ones_ref, o_ref, lse_ref):
    nres, rows, _ = cur_ref.shape
    nblk = rows // DIL_BAND
    gw = DIL_GROUP_WIDTH
    nh = DIL_HEADS_PER_GROUP
    step = pl.program_id(2)
    lane_head = lax.broadcasted_iota(jnp.int32, (1, gw), 1) // DIL_HEAD_DIM
    hms = [lane_head == h for h in range(nh)]
    hmuls = [hm.astype(BF16) for hm in hms]
    ones_bd = ones_ref[...]

    def blocks(items):
        r0s = [n * DIL_BAND if isinstance(n, int) else pl.multiple_of(n * DIL_BAND, DIL_BAND) for _, n in items]
        kvs = []
        for (res, n), r0 in zip(items, r0s):
            if isinstance(n, int) and n == 0:
                kvs.append(jnp.concatenate([halo_ref[res, :, gw:], cur_ref[res, :DIL_BAND, gw:]], axis=0))
            else:
                kvs.append(cur_ref[res, pl.ds(r0 - DIL_BAND, 2 * DIL_BAND), gw:])
        ss = []
        for (res, _), r0, kvb in zip(items, r0s, kvs):
            q = cur_ref[res, pl.ds(r0, DIL_BAND), :gw]
            qs = jnp.concatenate([q * hmul for hmul in hmuls], axis=0)
            ss.append(_dot_nt(qs, kvb[:, :gw]))
        ps, ms = [], []
        for (_, n), s in zip(items, ss):
            if isinstance(n, int) and n > 0:
                first = 0
            else:
                first = (step * nblk + n == 0).astype(jnp.int32)
            p_h, m_h = [], []
            for h in range(nh):
                sh = s[h * DIL_BAND:(h + 1) * DIL_BAND, :] + bias_ref[first, h]
                m = jnp.max(sh, axis=-1, keepdims=True)
                p_h.append(jnp.exp(sh - m).astype(BF16))
                m_h.append(m)
            ps.append(jnp.concatenate(p_h, axis=1))
            ms.append(m_h)
        accs, dens = [], []
        for p, kvb in zip(ps, kvs):
            v = kvb[:, gw:]
            v_bd = jnp.concatenate([v * hmul for hmul in hmuls], axis=0)
            accs.append(_dot(p, v_bd))
            dens.append(_dot(p, ones_bd))
        for (res, _), r0, acc, den, m_h in zip(items, r0s, accs, dens, ms):
            m_b = m_h[0]
            for h in range(1, nh):
                m_b = jnp.where(hms[h], m_h[h], m_b)
            o_ref[res, pl.ds(r0, DIL_BAND), :] = acc / den
            lse_ref[res, pl.ds(r0, DIL_BAND), :] = m_b + jnp.log(den)

    group = DIL_BLOCKS_PER_GROUP
    if nres > 1:
        items = [(res, n) for res in range(nres) for n in range(nblk)]
        for g0 in range(0, len(items), group):
            blocks(items[g0:g0 + group])
    else:
        blocks([(0, n) for n in range(group)])

        def body(g, carry):
            blocks([(0, g * group + u) for u in range(group)])
            return carry

        lax.fori_loop(1, nblk // group, body, 0)


def _dil_attn(qkv, bias, ones_bd):
    b, dil, length, gw3 = qkv.shape
    rows = min(length, DIL_STEP_ROWS)
    nres = DIL_STEP_ROWS // rows
    rb = rows // DIL_BAND
    out_shape = (jax.ShapeDtypeStruct((b, dil, length, DIL_GROUP_WIDTH), F32),) * 2
    blk = lambda w: pl.BlockSpec((None, nres, rows, w), lambda bi, ri, i: (bi, ri, i, 0))
    halo = pl.BlockSpec((None, nres, DIL_BAND, gw3),
                        lambda bi, ri, i: (bi, ri, jnp.maximum(i * rb - 1, 0), 0))
    return pl.pallas_call(
        _dil_attn_kernel,
        grid=(b, dil // nres, length // rows),
        in_specs=[blk(gw3), halo, pl.BlockSpec(bias.shape, lambda bi, ri, i: (0, 0, 0, 0)),
                  pl.BlockSpec(ones_bd.shape, lambda bi, ri, i: (0, 0))],
        out_specs=(blk(DIL_GROUP_WIDTH), blk(DIL_GROUP_WIDTH)), out_shape=out_shape,
        compiler_params=_cparams(3), name=f"dil_attn_d{dil}",
    )(qkv, qkv, bias, ones_bd)


def _sb_attn_kernel(q_ref, k_ref, vt_ref, ntri_ref, o_ref, acc_ref):
    tk = tq = SB_TILE
    i = pl.program_id(1)
    ntri = ntri_ref[...]
    row = lax.broadcasted_iota(jnp.int32, (tk, tq), 0)
    col = lax.broadcasted_iota(jnp.int32, (tk, tq), 1)
    sign = jnp.uint32(0x80000000)
    heads = range(SB_HEADS)
    hsl = [slice(h * SB_HEAD_DIM, (h + 1) * SB_HEAD_DIM) for h in heads]

    def tiles(specs, runs, first):
        chains = [(j, mask, h) for j, mask in specs for h in heads]
        kq = lambda j, h: _dot_nt(k_ref[pl.ds(pl.multiple_of(j * tk, tk), tk), hsl[h]], q_ref[:, hsl[h]])
        zs = [kq(j, h) for j, _, h in chains]
        his, z0s, lsig = [], [], []
        for (_, mask, _), z in zip(chains, zs):
            neg_abs = lax.bitcast_convert_type(lax.bitcast_convert_type(z, jnp.uint32) | sign, F32)
            sp = jnp.maximum(z, 0.0) + jnp.log(1.0 + jnp.exp2(neg_abs)) * LOG2E
            if mask is not None:
                sp = jnp.where(mask, sp, 0.0)
            his.append(sp.astype(BF16))
            z0s.append(z[0:1, :])
            lsig.append(z - sp)
        ds = [ls + _dot(ntri, hi) for ls, hi in zip(lsig, his)]
        runs = list(runs)
        ws = []
        for (_, mask, h), z0, d in zip(chains, z0s, ds):
            w = jnp.exp2(d - runs[h])
            if mask is not None:
                w = jnp.where(mask, w, 0.0)
            ws.append(w.astype(BF16))
            runs[h] = runs[h] + (z0 - d[0:1, :])
        pvs = [_dot(vt_ref[j, hsl[h], :], w) for (j, _, h), w in zip(chains, ws)]
        for h in heads:
            total = pvs[h]
            for t in range(1, len(specs)):
                total = total + pvs[t * SB_HEADS + h]
            if first:
                acc_ref[h] = total
            else:
                acc_ref[h] += total
        return tuple(runs)

    zero_runs = (jnp.zeros((1, tq), F32),) * SB_HEADS
    causal = row < col

    def min_run(runs):
        return jnp.min(jnp.minimum(jnp.minimum(runs[0], runs[1]), jnp.minimum(runs[2], runs[3])))

    @pl.when(i == 0)
    def _():
        tiles([(0, causal)], zero_runs, True)

    @pl.when(i > 0)
    def _():
        runs = tiles([(i, causal), (i - 1, None)], zero_runs, True)
        nfull = i - 1

        def cond(carry):
            jj, mr, _ = carry
            return jnp.logical_and(jj < nfull, mr < SB_UNDERFLOW)

        def body(carry):
            jj, _, runs = carry
            runs = tiles([(nfull - 1 - jj, None)], runs, False)
            return jj + 1, min_run(runs), runs

        lax.while_loop(cond, body, (jnp.int32(0), min_run(runs), runs))

    for h in heads:
        o_ref[:, hsl[h]] = acc_ref[h].T.astype(o_ref.dtype)


def _sb_attn(q, k, vt, ntri):
    b, s, _ = q.shape
    tq = tk = SB_TILE
    return pl.pallas_call(
        _sb_attn_kernel,
        grid=(b, s // tq),
        in_specs=[
            pl.BlockSpec((None, tq, SB_WIDTH), lambda bi, i: (bi, i, 0)),
            pl.BlockSpec((None, s, SB_WIDTH), lambda bi, i: (bi, 0, 0)),
            pl.BlockSpec((None, s // tk, SB_WIDTH, tk), lambda bi, i: (bi, 0, 0, 0)),
            pl.BlockSpec((tk, tk), lambda bi, i: (0, 0)),
        ],
        out_specs=pl.BlockSpec((None, tq, SB_WIDTH), lambda bi, i: (bi, i, 0)),
        out_shape=jax.ShapeDtypeStruct((b, s, SB_WIDTH), BF16),
        scratch_shapes=[pltpu.VMEM((SB_HEADS, SB_HEAD_DIM, tq), F32)],
        compiler_params=_cparams(2), name="sb_attn",
    )(q, k, vt, ntri)


def _gelu_tanh(x):
    return 0.5 * x * (1.0 + jnp.tanh(math.sqrt(2.0 / math.pi) * (x + 0.044715 * (x * x * x))))


def _s5_kernel(u_ref, lr_ref, li_ref, wb_ref, wc_ref, d_ref, wglu_ref, out_ref, bu_ref, h_ref, *, steps):
    cs = S5_CHUNK_STATE

    @pl.when(pl.program_id(0) == 0)
    def _():
        h_ref[...] = jnp.zeros_like(h_ref)

    nb = h_ref.shape[0]
    u = pltpu.einshape("btc->tbc", u_ref[...]).reshape(steps * nb, S5_WIDTH)
    ub = u.astype(BF16)
    for c in range(S5_CHUNKS):
        bu_ref[:, 2 * cs * c:2 * cs * (c + 1)] = _dot(ub[:, S5_CHUNK_CH * c:S5_CHUNK_CH * (c + 1)], wb_ref[c])

    for c0 in range(0, S5_CHUNKS, 2):
        lam = [(jnp.broadcast_to(lr_ref[c], (nb, cs)), jnp.broadcast_to(li_ref[c], (nb, cs)))
               for c in (c0, c0 + 1)]
        offs = [2 * cs * c for c in (c0, c0 + 1)]

        def step(t, carry):
            r = pl.multiple_of(t * nb, nb)
            new = []
            for (lr, li), off, (hr, hi) in zip(lam, offs, carry):
                br = bu_ref[pl.ds(r, nb), off:off + cs]
                bi = bu_ref[pl.ds(r, nb), off + cs:off + 2 * cs]
                nhr = lr * hr - li * hi + br
                nhi = lr * hi + li * hr + bi
                bu_ref[pl.ds(r, nb), off:off + cs] = nhr
                bu_ref[pl.ds(r, nb), off + cs:off + 2 * cs] = nhi
                new.append((nhr, nhi))
            return tuple(new)

        init = tuple((h_ref[:, off:off + cs], h_ref[:, off + cs:off + 2 * cs]) for off in offs)
        fin = lax.fori_loop(0, steps, step, init, unroll=2)
        for off, (hr, hi) in zip(offs, fin):
            h_ref[:, off:off + cs] = hr
            h_ref[:, off + cs:off + 2 * cs] = hi

    y = jnp.concatenate(
        [_dot(bu_ref[:, 2 * cs * c:2 * cs * (c + 1)].astype(BF16), wc_ref[c]) for c in range(S5_CHUNKS)],
        axis=1)
    y = _gelu_tanh(y + d_ref[...] * u)
    gl = _dot(y.astype(BF16), wglu_ref[...])
    out = gl[:, :S5_WIDTH] * jax.nn.sigmoid(gl[:, S5_WIDTH:])
    out_ref[...] = pltpu.einshape("tbc->btc", out.reshape(steps, nb, S5_WIDTH))


def _s5(u, layer, lr, li, wb, wc, d, wglu, *, steps):
    batch, s, _ = u.shape
    rows = steps * batch
    const = lambda a: _layer_spec(a, layer)
    blk = pl.BlockSpec((batch, steps, S5_WIDTH), lambda i: (0, i, 0))
    return pl.pallas_call(
        functools.partial(_s5_kernel, steps=steps),
        grid=(s // steps,),
        in_specs=[blk, const(lr), const(li), const(wb), const(wc), const(d), const(wglu)],
        out_specs=blk,
        out_shape=jax.ShapeDtypeStruct((batch, s, S5_WIDTH), F32),
        scratch_shapes=[pltpu.VMEM((rows, 2 * S5_CHUNKS * S5_CHUNK_STATE), F32),
                        pltpu.VMEM((batch, 2 * S5_CHUNKS * S5_CHUNK_STATE), F32)],
        compiler_params=_cparams(1), name="s5_mixer",
    )(u, lr, li, wb, wc, d, wglu)


def _merge_kernel(x_ref, g_ref, up_ref, o0_ref, l0_ref, o1_ref, l1_ref, o2_ref, l2_ref, ysb_ref, ys5_ref,
                  pw_ref, ps_ref, wg_ref, wb_ref, wo_ref, out_ref, ext_ref, lvl_ref, no1_ref, nl1_ref, no2_ref,
                  nl2_ref, *, tm):
    ti = pl.program_id(1)
    x = x_ref[...]
    xn = _rms(x, g_ref[...]).astype(BF16)

    @pl.when(ti == 0)
    def _():
        ext_ref[:POOL_HALO, :] = jnp.zeros((POOL_HALO, POOL_WIDTH), F32)
        ext_ref[pl.ds(tm + POOL_HALO, POOL_HALO), :] = jnp.zeros((POOL_HALO, POOL_WIDTH), F32)
        lvl_ref[:, pl.ds(tm + POOL_HALO, POOL_HALO), :] = jnp.zeros((2, POOL_HALO, POOL_GROUP), F32)

    ext_ref[pl.ds(POOL_HALO, tm), :] = up_ref[...]
    pos = ti * tm + lax.broadcasted_iota(jnp.int32, (tm, 1), 0)
    ys = []
    for gi, w in enumerate(POOL_WINDOWS):
        c0, c1 = gi * POOL_GROUP, (gi + 1) * POOL_GROUP
        src, have = ext_ref, 1
        while have < w:
            cols = slice(c0, c1) if src is ext_ref else slice(0, POOL_GROUP)
            nxt = src[pl.ds(have, tm + POOL_HALO), cols] + src[pl.ds(0, tm + POOL_HALO), cols]
            dst = lvl_ref.at[have.bit_length() % 2]
            dst[pl.ds(0, tm + POOL_HALO), :] = nxt
            src, have = dst, 2 * have
        off = have - 1
        cols = slice(c0, c1) if src is ext_ref else slice(0, POOL_GROUP)
        acc = src[pl.ds(POOL_HALO - off, tm), cols]
        cnt = jnp.minimum(pos + 1, w).astype(F32)
        p = acc / cnt - ext_ref[pl.ds(POOL_HALO, tm), c0:c1]
        ys.append(_dot(p.astype(BF16), pw_ref[gi]))
    y_pool = jnp.concatenate(ys, axis=1) * ps_ref[...]
    ext_ref[:POOL_HALO, :] = ext_ref[pl.ds(tm, POOL_HALO), :]

    nslab = DIL_GROUP_WIDTH // LANES

    def to_token_order(src_ref, dst_ref):
        dil = src_ref.shape[0]
        for r in range(dil):
            for sl in range(nslab):
                dst_ref[sl, pl.ds(r, tm // dil, stride=dil), :] = src_ref[r, :, sl * LANES:(sl + 1) * LANES]
        return jnp.concatenate([dst_ref[sl] for sl in range(nslab)], axis=1)

    o1, l1 = to_token_order(o1_ref, no1_ref), to_token_order(l1_ref, nl1_ref)
    o2, l2 = to_token_order(o2_ref, no2_ref), to_token_order(l2_ref, nl2_ref)
    l0 = l0_ref[...]
    m = jnp.maximum(jnp.maximum(l0, l1), l2)
    e0, e1, e2 = jnp.exp(l0 - m), jnp.exp(l1 - m), jnp.exp(l2 - m)
    y_dil = (e0 * o0_ref[...] + e1 * o1 + e2 * o2) / (e0 + e1 + e2)

    branches = (y_pool.astype(BF16), y_dil.astype(BF16), ysb_ref[...], ys5_ref[...].astype(BF16))
    merged = jnp.zeros((tm, x.shape[1]), F32)
    for bi, yb in enumerate(branches):
        gate = jax.nn.sigmoid(_dot(xn, wg_ref[bi]))
        merged = merged + gate * _dot(yb, wb_ref[BRANCH_ROWS[bi]:BRANCH_ROWS[bi + 1], :])
    out_ref[...] = x + _dot(merged.astype(BF16), wo_ref[...])


def _merge(x, layer, g, up, o0, l0, o1, l1, o2, l2, ysb, ys5, pw, ps, wg, wb, wo, *, tm):
    b, s, d = x.shape
    dil1, dil2 = o1.shape[1], o2.shape[1]
    gw = DIL_GROUP_WIDTH
    tok = lambda w: pl.BlockSpec((None, tm, w), lambda bi, ti: (bi, ti, 0))
    res = lambda dil: pl.BlockSpec((None, dil, tm // dil, gw), lambda bi, ti: (bi, 0, ti, 0))
    const = lambda a: _layer_spec(a, layer)
    in_specs = [tok(d), const(g), tok(POOL_WIDTH), tok(gw), tok(gw), res(dil1), res(dil1), res(dil2), res(dil2),
                tok(SB_WIDTH), tok(S5_WIDTH),
                const(pw), const(ps), const(wg), const(wb), const(wo)]
    return pl.pallas_call(
        functools.partial(_merge_kernel, tm=tm),
        grid=(b, s // tm), in_specs=in_specs, out_specs=tok(d),
        out_shape=jax.ShapeDtypeStruct((b, s, d), F32),
        scratch_shapes=[pltpu.VMEM((tm + 2 * POOL_HALO, POOL_WIDTH), F32),
                        pltpu.VMEM((2, tm + 2 * POOL_HALO, POOL_GROUP), F32)]
        + [pltpu.VMEM((gw // LANES, tm, LANES), F32)] * 4,
        compiler_params=_cparams(2), name="gated_merge",
    )(x, g, up, o0, l0, o1, l1, o2, l2, ysb, ys5, pw, ps, wg, wb, wo)


def _ffn_kernel(x_ref, g_ref, wup_ref, wdn_ref, fg_ref, out_ref, *, hidden, final):
    x = x_ref[...]
    hn = _rms(x, g_ref[...]).astype(BF16)
    gu = _dot(hn, wup_ref[...])
    h = (jax.nn.silu(gu[:, :hidden]) * gu[:, hidden:]).astype(BF16)
    y = x + _dot(h, wdn_ref[...])
    if final:
        y = _rms(y, fg_ref[...])
    out_ref[...] = y


def _ffn(x2d, layer, g, wup, wdn, fg, *, tm, final):
    n, d = x2d.shape
    hidden = wdn.shape[-2]
    const = lambda a: _layer_spec(a, layer, pipeline_mode=pl.Buffered(1))
    return pl.pallas_call(
        functools.partial(_ffn_kernel, hidden=hidden, final=final),
        grid=(n // tm,),
        in_specs=[pl.BlockSpec((tm, d), lambda i: (i, 0)), const(g), const(wup), const(wdn),
                  pl.BlockSpec(fg.shape, lambda i: (0, 0))],
        out_specs=pl.BlockSpec((tm, d), lambda i: (i, 0)),
        out_shape=jax.ShapeDtypeStruct((n, d), F32),
        compiler_params=_cparams(1), name="swiglu_final" if final else "swiglu",
    )(x2d, g, wup, wdn, fg)


def _t5_bucket(dist):
    exact = REL_BUCKETS // 2
    df = jnp.maximum(dist, 1).astype(F32)
    large = exact + (jnp.log(df / exact) / math.log(REL_MAX_DIST / exact)
                     * (REL_BUCKETS - exact)).astype(jnp.int32)
    large = jnp.minimum(large, REL_BUCKETS - 1)
    return jnp.where(dist < exact, dist, large)


def _dil_bias(rel_bias_g, dil):
    band = DIL_BAND
    i = jnp.arange(band)[:, None]
    c = jnp.arange(2 * band)[None, :]
    dist_sub = band + i - c
    in_band = (dist_sub >= 0) & (dist_sub <= band)
    buckets = _t5_bucket(jnp.clip(dist_sub, 0, band) * dil)
    onehot = (buckets[None] == jnp.arange(REL_BUCKETS)[:, None, None]).astype(F32)
    bias = jnp.einsum('kh,kic->hic', rel_bias_g.astype(F32), onehot, precision=lax.Precision.HIGHEST)
    bias = jnp.where(in_band[None], bias, NEG_INF)
    first = jnp.where((c >= band)[None], bias, NEG_INF)
    return jnp.stack([bias, first])


def _dil_weights(w_dil, group):
    hw = DIL_HEADS * DIL_HEAD_DIM
    cols = [w_dil[..., i * hw + group * DIL_GROUP_WIDTH:i * hw + (group + 1) * DIL_GROUP_WIDTH] for i in range(3)]
    return jnp.concatenate(cols, axis=-1).astype(BF16)


def _s5_params(a_re, a_im, log_dt, b_re, b_im, c_re, c_im):
    lam = lax.complex(a_re.astype(F32), a_im.astype(F32))
    dt = jnp.exp(log_dt.astype(F32))[:, None]
    lam_bar = jnp.exp(lam * dt)
    b_bar = ((lam_bar - 1.0) / lam)[:, :, None] * lax.complex(b_re.astype(F32), b_im.astype(F32))
    gpc = S5_GROUPS // S5_CHUNKS
    eye = jnp.eye(gpc, dtype=F32)

    def pack_b(bm):
        bm = bm.reshape(S5_CHUNKS, gpc, S5_STATE, S5_CH)
        return jnp.einsum('cgpk,gh->cgkhp', bm, eye).reshape(S5_CHUNKS, S5_CHUNK_CH, S5_CHUNK_STATE)

    def pack_c(cm):
        cm = cm.reshape(S5_CHUNKS, gpc, S5_CH, S5_STATE)
        return jnp.einsum('cgkp,gh->cgphk', cm, eye).reshape(S5_CHUNKS, S5_CHUNK_STATE, S5_CHUNK_CH)

    wb = jnp.concatenate([pack_b(jnp.real(b_bar)), pack_b(jnp.imag(b_bar))], axis=2).astype(BF16)
    wc = jnp.concatenate([pack_c(c_re.astype(F32)), pack_c(-c_im.astype(F32))], axis=1).astype(BF16)
    lr = jnp.real(lam_bar).reshape(S5_CHUNKS, 1, S5_CHUNK_STATE)
    li = jnp.imag(lam_bar).reshape(S5_CHUNKS, 1, S5_CHUNK_STATE)
    return lr, li, wb, wc


def kernel(x, attn_norm_g, w_in, pool_w, pool_scale, rel_bias, s5_a_re, s5_a_im, s5_log_dt, s5_b_re, s5_b_im,
           s5_c_re, s5_c_im, s5_d, s5_w_glu, w_branch, w_gate, w_out, ffn_norm_g, w_up, w_down, final_norm_g):
    b, s, d = x.shape
    depth = w_in.shape[0]
    o1 = POOL_WIDTH
    o2 = o1 + 3 * DIL_HEADS * DIL_HEAD_DIM
    o3 = o2 + 3 * SB_WIDTH

    t = SB_TILE
    ntri = -(jnp.arange(t)[None, :] > jnp.arange(t)[:, None]).astype(BF16)
    biases = [_dil_bias(rel_bias[:, g * DIL_HEADS_PER_GROUP:(g + 1) * DIL_HEADS_PER_GROUP], dil)
              for g, (_, dil) in enumerate(DIL_PAIRS)]
    key_head = jnp.arange(DIL_HEADS_PER_GROUP * 2 * DIL_BAND) // (2 * DIL_BAND)
    ones_bd = (key_head[:, None] == (jnp.arange(DIL_GROUP_WIDTH) // DIL_HEAD_DIM)[None, :]).astype(BF16)
    fg = final_norm_g.reshape(1, d).astype(F32)

    w_dil = w_in[:, :, o1:o2]
    w_sb = w_in[:, :, o2:o3]
    in_w = (attn_norm_g.reshape(depth, 1, d).astype(F32), w_in[:, :, :o1].astype(BF16),
            _dil_weights(w_dil, 0), _dil_weights(w_dil, 1), _dil_weights(w_dil, 2),
            w_sb[:, :, :SB_WIDTH].astype(BF16), w_sb[:, :, SB_WIDTH:2 * SB_WIDTH].astype(BF16),
            jnp.swapaxes(w_sb[:, :, 2 * SB_WIDTH:], 1, 2).astype(BF16), w_in[:, :, o3:].astype(BF16))
    s5_w = jax.vmap(_s5_params)(s5_a_re, s5_a_im, s5_log_dt, s5_b_re, s5_b_im, s5_c_re, s5_c_im) + (
        s5_d.reshape(depth, 1, S5_WIDTH).astype(F32), s5_w_glu.astype(BF16))
    merge_w = (pool_w.astype(BF16), pool_scale.reshape(depth, 1, POOL_WIDTH).astype(F32),
               w_gate.astype(BF16), w_branch.astype(BF16), w_out.astype(BF16))
    ffn_w = (ffn_norm_g.reshape(depth, 1, d).astype(F32), w_up.astype(BF16), w_down.astype(BF16))

    for l in range(depth):
        up, d0, d1, d2, q, k, vt, u_s5 = _in_proj(x, l, *in_w, tm=IN_PROJ_ROWS)

        oa0, la0 = _dil_attn(d0.reshape(b, 1, s, d0.shape[-1]), biases[0], ones_bd)
        oa1, la1 = _dil_attn(d1, biases[1], ones_bd)
        oa2, la2 = _dil_attn(d2, biases[2], ones_bd)
        y_sb = _sb_attn(q, k, vt, ntri)
        y_s5 = _s5(u_s5, l, *s5_w, steps=S5_STEPS)

        x = _merge(x, l, in_w[0], up, oa0.reshape(b, s, -1), la0.reshape(b, s, -1), oa1, la1, oa2, la2,
                   y_sb, y_s5, *merge_w, tm=MERGE_ROWS)
        x = _ffn(x.reshape(b * s, d), l, *ffn_w, fg, tm=FFN_ROWS, final=(l == depth - 1)).reshape(b, s, d)
    return x
```

```python
import functools
import math

import jax
import jax.numpy as jnp
from jax import lax
from jax.experimental import pallas as pl
from jax.experimental.pallas import tpu as pltpu

F32 = jnp.float32
BF16 = jnp.bfloat16

EPS = 1e-6
POOL_WINDOWS = (2, 4, 8, 16)
POOL_GROUP = 128
POOL_WIDTH = 512
POOL_HALO = 16
DIL_PAIRS = ((128, 1), (512, 4), (2048, 16))
DIL_BAND = 128
DIL_HEADS_PER_GROUP = 4
DIL_HEADS = 12
DIL_HEAD_DIM = 64
DIL_GROUP_WIDTH = DIL_HEADS_PER_GROUP * DIL_HEAD_DIM
DIL_STEP_ROWS = 1024
DIL_BLOCKS_PER_GROUP = 4
REL_BUCKETS = 32
REL_MAX_DIST = 2048
SB_HEADS = 4
SB_HEAD_DIM = 128
SB_WIDTH = 512
SB_TILE = 256
S5_WIDTH = 512
S5_CH = 16
S5_GROUPS = 32
S5_STATE = 64
S5_CHUNKS = 4
S5_CHUNK_CH = S5_WIDTH // S5_CHUNKS
S5_CHUNK_STATE = (S5_GROUPS // S5_CHUNKS) * S5_STATE
BRANCH_ROWS = (0, 512, 768, 1280, 1792)
NEG_INF = -1e30
SB_UNDERFLOW = 160.0
LANES = 128
LOG2E = math.log2(math.e)

IN_PROJ_ROWS = 1024
MERGE_ROWS = 256
FFN_ROWS = 512
S5_STEPS = 128

VMEM_LIMIT = 56 * 1024 * 1024


def _cparams(n_axes):
    return pltpu.CompilerParams(dimension_semantics=("arbitrary",) * n_axes,
                                vmem_limit_bytes=VMEM_LIMIT)


def _layer_spec(a, layer, **kw):
    tail = (0,) * (a.ndim - 1)
    return pl.BlockSpec((None,) + a.shape[1:], lambda *_: (layer,) + tail, **kw)


def _dot(a, b):
    return jnp.dot(a, b, preferred_element_type=F32)


def _dot_nt(a, b):
    return lax.dot_general(a, b, (((1,), (1,)), ((), ())), preferred_element_type=F32)


def _rms(x, g):
    ms = jnp.mean(x * x, axis=-1, keepdims=True)
    return x * lax.rsqrt(ms + EPS) * g


def _in_proj_kernel(x_ref, g_ref, wpool_ref, wd0_ref, wd1_ref, wd2_ref, wq_ref, wk_ref, wvt_ref,
                    ws5_ref, pool_ref, d0_ref, d1_ref, d2_ref, q_ref, k_ref, vt_ref, s5_ref,
                    scr_ref, *, tm, sb_scale):
    xn = _rms(x_ref[...], g_ref[...]).astype(BF16)
    pool_ref[...] = _dot(xn, wpool_ref[...])
    s5_ref[...] = _dot(xn, ws5_ref[...])

    def dil_proj(w_ref):
        q = _dot(xn, w_ref[:, :DIL_GROUP_WIDTH]) * (1.0 / math.sqrt(DIL_HEAD_DIM))
        kv = _dot(xn, w_ref[:, DIL_GROUP_WIDTH:])
        return q, kv

    q0, kv0 = dil_proj(wd0_ref)
    d0_ref[:, :DIL_GROUP_WIDTH] = q0.astype(BF16)
    d0_ref[:, DIL_GROUP_WIDTH:] = kv0.astype(BF16)
    for w_ref, out_ref, dil in ((wd1_ref, d1_ref, DIL_PAIRS[1][1]), (wd2_ref, d2_ref, DIL_PAIRS[2][1])):
        qg, kvg = dil_proj(w_ref)
        qkv = jnp.concatenate([qg, kvg], axis=1)
        for sl in range(qkv.shape[1] // LANES):
            scr_ref[sl] = qkv[:, sl * LANES:(sl + 1) * LANES]
        for r in range(dil):
            for sl in range(qkv.shape[1] // LANES):
                out_ref[r, :, sl * LANES:(sl + 1) * LANES] = (
                    scr_ref[sl, pl.ds(r, tm // dil, stride=dil), :].astype(BF16))

    q_ref[...] = (_dot(xn, wq_ref[...]) * sb_scale).astype(BF16)
    k_ref[...] = _dot(xn, wk_ref[...]).astype(BF16)
    vt = _dot_nt(wvt_ref[...], xn).astype(BF16)
    for jb in range(tm // SB_TILE):
        vt_ref[jb] = vt[:, jb * SB_TILE:(jb + 1) * SB_TILE]


def _in_proj(x, layer, g, wpool, wd0, wd1, wd2, wq, wk, wvt, ws5, *, tm):
    b, s, d = x.shape
    dil1, dil2 = DIL_PAIRS[1][1], DIL_PAIRS[2][1]
    gw = 3 * DIL_GROUP_WIDTH
    const = lambda a: _layer_spec(a, layer, pipeline_mode=pl.Buffered(1))
    out_shape = (
        jax.ShapeDtypeStruct((b, s, POOL_WIDTH), F32),
        jax.ShapeDtypeStruct((b, s, gw), BF16),
        jax.ShapeDtypeStruct((b, dil1, s // dil1, gw), BF16),
        jax.ShapeDtypeStruct((b, dil2, s // dil2, gw), BF16),
        jax.ShapeDtypeStruct((b, s, SB_WIDTH), BF16),
        jax.ShapeDtypeStruct((b, s, SB_WIDTH), BF16),
        jax.ShapeDtypeStruct((b, s // SB_TILE, SB_WIDTH, SB_TILE), BF16),
        jax.ShapeDtypeStruct((b, s, S5_WIDTH), F32),
    )
    out_specs = (
        pl.BlockSpec((None, tm, POOL_WIDTH), lambda bi, ti: (bi, ti, 0)),
        pl.BlockSpec((None, tm, gw), lambda bi, ti: (bi, ti, 0)),
        pl.BlockSpec((None, dil1, tm // dil1, gw), lambda bi, ti: (bi, 0, ti, 0)),
        pl.BlockSpec((None, dil2, tm // dil2, gw), lambda bi, ti: (bi, 0, ti, 0)),
        pl.BlockSpec((None, tm, SB_WIDTH), lambda bi, ti: (bi, ti, 0)),
        pl.BlockSpec((None, tm, SB_WIDTH), lambda bi, ti: (bi, ti, 0)),
        pl.BlockSpec((None, tm // SB_TILE, SB_WIDTH, SB_TILE), lambda bi, ti: (bi, ti, 0, 0)),
        pl.BlockSpec((None, tm, S5_WIDTH), lambda bi, ti: (bi, ti, 0)),
    )
    in_specs = [
        pl.BlockSpec((None, tm, d), lambda bi, ti: (bi, ti, 0)),
        const(g), const(wpool), const(wd0), const(wd1), const(wd2), const(wq), const(wk), const(wvt), const(ws5),
    ]
    sb_scale = LOG2E / math.sqrt(SB_HEAD_DIM)
    return pl.pallas_call(
        functools.partial(_in_proj_kernel, tm=tm, sb_scale=sb_scale),
        grid=(b, s // tm), in_specs=in_specs, out_specs=out_specs, out_shape=out_shape,
        scratch_shapes=[pltpu.VMEM((gw // LANES, tm, LANES), F32)],
        compiler_params=_cparams(2), name="in_proj",
    )(x, g, wpool, wd0, wd1, wd2, wq, wk, wvt, ws5)


def _dil_attn_kernel(cur_ref, halo_ref, bias_ref, ones_ref, o_ref, lse_ref):
    nres, rows, _ = cur_ref.shape
    nblk = rows // DIL_BAND
    gw = DIL_GROUP_WIDTH
    nh = DIL_HEADS_PER_GROUP
    step = pl.program_id(2)
    lane_head = lax.broadcasted_iota(jnp.int32, (1, gw), 1) // DIL_HEAD_DIM
    hms = [lane_head == h for h in range(nh)]
    hmuls = [hm.astype(BF16) for hm in hms]
    ones_bd = ones_ref[...]

    def blocks(items):
        r0s = [n * DIL_BAND if isinstance(n, int) else pl.multiple_of(n * DIL_BAND, DIL_BAND) for _, n in items]
        kvs = []
        for (res, n), r0 in zip(items, r0s):
            if isinstance(n, int) and n == 0:
                kvs.append(jnp.concatenate([halo_ref[res, :, gw:], cur_ref[res, :DIL_BAND, gw:]], axis=0))
            else:
                kvs.append(cur_ref[res, pl.ds(r0 - DIL_BAND, 2 * DIL_BAND), gw:])
        ss = []
        for (res, _), r0, kvb in zip(items, r0s, kvs):
            q = cur_ref[res, pl.ds(r0, DIL_BAND), :gw]
            qs = jnp.concatenate([q * hmul for hmul in hmuls], axis=0)
            ss.append(_dot_nt(qs, kvb[:, :gw]))
        ps, ms = [], []
        for (_, n), s in zip(items, ss):
            if isinstance(n, int) and n > 0:
                first = 0
            else:
                first = (step * nblk + n == 0).astype(jnp.int32)
            p_h, m_h = [], []
            for h in range(nh):
                sh = s[h * DIL_BAND:(h + 1) * DIL_BAND, :] + bias_ref[first, h]
                m = jnp.max(sh, axis=-1, keepdims=True)
                p_h.append(jnp.exp(sh - m).astype(BF16))
                m_h.append(m)
            ps.append(jnp.concatenate(p_h, axis=1))
            ms.append(m_h)
        accs, dens = [], []
        for p, kvb in zip(ps, kvs):
            v = kvb[:, gw:]
            v_bd = jnp.concatenate([v * hmul for hmul in hmuls], axis=0)
            accs.append(_dot(p, v_bd))
            dens.append(_dot(p, ones_bd))
        for (res, _), r0, acc, den, m_h in zip(items, r0s, accs, dens, ms):
            m_b = m_h[0]
            for h in range(1, nh):
                m_b = jnp.where(hms[h], m_h[h], m_b)
            o_ref[res, pl.ds(r0, DIL_BAND), :] = acc / den
            lse_ref[res, pl.ds(r0, DIL_BAND), :] = m_b + jnp.log(den)

    group = DIL_BLOCKS_PER_GROUP
    if nres > 1:
        items = [(res, n) for res in range(nres) for n in range(nblk)]
        for g0 in range(0, len(items), group):
            blocks(items[g0:g0 + group])
    else:
        blocks([(0, n) for n in range(group)])

        def body(g, carry):
            blocks([(0, g * group + u) for u in range(group)])
            return carry

        lax.fori_loop(1, nblk // group, body, 0)


def _dil_attn(qkv, bias, ones_bd):
    b, dil, length, gw3 = qkv.shape
    rows = min(length, DIL_STEP_ROWS)
    nres = DIL_STEP_ROWS // rows
    rb = rows // DIL_BAND
    out_shape = (jax.ShapeDtypeStruct((b, dil, length, DIL_GROUP_WIDTH), F32),) * 2
    blk = lambda w: pl.BlockSpec((None, nres, rows, w), lambda bi, ri, i: (bi, ri, i, 0))
    halo = pl.BlockSpec((None, nres, DIL_BAND, gw3),
                        lambda bi, ri, i: (bi, ri, jnp.maximum(i * rb - 1, 0), 0))
    return pl.pallas_call(
        _dil_attn_kernel,
        grid=(b, dil // nres, length // rows),
        in_specs=[blk(gw3), halo, pl.BlockSpec(bias.shape, lambda bi, ri, i: (0, 0, 0, 0)),
                  pl.BlockSpec(ones_bd.shape, lambda bi, ri, i: (0, 0))],
        out_specs=(blk(DIL_GROUP_WIDTH), blk(DIL_GROUP_WIDTH)), out_shape=out_shape,
        compiler_params=_cparams(3), name=f"dil_attn_d{dil}",
    )(qkv, qkv, bias, ones_bd)


def _sb_attn_kernel(q_ref, k_ref, vt_ref, ntri_ref, o_ref, acc_ref):
    tk = tq = SB_TILE
    i = pl.program_id(1)
    ntri = ntri_ref[...]
    row = lax.broadcasted_iota(jnp.int32, (tk, tq), 0)
    col = lax.broadcasted_iota(jnp.int32, (tk, tq), 1)
    sign = jnp.uint32(0x80000000)
    heads = range(SB_HEADS)
    hsl = [slice(h * SB_HEAD_DIM, (h + 1) * SB_HEAD_DIM) for h in heads]

    def tiles(specs, runs, first):
        chains = [(j, mask, h) for j, mask in specs for h in heads]
        kq = lambda j, h: _dot_nt(k_ref[pl.ds(pl.multiple_of(j * tk, tk), tk), hsl[h]], q_ref[:, hsl[h]])
        zs = [kq(j, h) for j, _, h in chains]
        his, z0s, lsig = [], [], []
        for (_, mask, _), z in zip(chains, zs):
            neg_abs = lax.bitcast_convert_type(lax.bitcast_convert_type(z, jnp.uint32) | sign, F32)
            sp = jnp.maximum(z, 0.0) + jnp.log(1.0 + jnp.exp2(neg_abs)) * LOG2E
            if mask is not None:
                sp = jnp.where(mask, sp, 0.0)
            his.append(sp.astype(BF16))
            z0s.append(z[0:1, :])
            lsig.append(z - sp)
        ds = [ls + _dot(ntri, hi) for ls, hi in zip(lsig, his)]
        runs = list(runs)
        ws = []
        for (_, mask, h), z0, d in zip(chains, z0s, ds):
            w = jnp.exp2(d - runs[h])
            if mask is not None:
                w = jnp.where(mask, w, 0.0)
            ws.append(w.astype(BF16))
            runs[h] = runs[h] + (z0 - d[0:1, :])
        pvs = [_dot(vt_ref[j, hsl[h], :], w) for (j, _, h), w in zip(chains, ws)]
        for h in heads:
            total = pvs[h]
            for t in range(1, len(specs)):
                total = total + pvs[t * SB_HEADS + h]
            if first:
                acc_ref[h] = total
            else:
                acc_ref[h] += total
        return tuple(runs)

    zero_runs = (jnp.zeros((1, tq), F32),) * SB_HEADS
    causal = row < col

    def min_run(runs):
        return jnp.min(jnp.minimum(jnp.minimum(runs[0], runs[1]), jnp.minimum(runs[2], runs[3])))

    @pl.when(i == 0)
    def _():
        tiles([(0, causal)], zero_runs, True)

    @pl.when(i > 0)
    def _():
        runs = tiles([(i, causal), (i - 1, None)], zero_runs, True)
        nfull = i - 1

        def cond(carry):
            jj, mr, _ = carry
            return jnp.logical_and(jj < nfull, mr < SB_UNDERFLOW)

        def body(carry):
            jj, _, runs = carry
            runs = tiles([(nfull - 1 - jj, None)], runs, False)
            return jj + 1, min_run(runs), runs

        lax.while_loop(cond, body, (jnp.int32(0), min_run(runs), runs))

    for h in heads:
        o_ref[:, hsl[h]] = acc_ref[h].T.astype(o_ref.dtype)


def _sb_attn(q, k, vt, ntri):
    b, s, _ = q.shape
    tq = tk = SB_TILE
    return pl.pallas_call(
        _sb_attn_kernel,
        grid=(b, s // tq),
        in_specs=[
            pl.BlockSpec((None, tq, SB_WIDTH), lambda bi, i: (bi, i, 0)),
            pl.BlockSpec((None, s, SB_WIDTH), lambda bi, i: (bi, 0, 0)),
            pl.BlockSpec((None, s // tk, SB_WIDTH, tk), lambda bi, i: (bi, 0, 0, 0)),
            pl.BlockSpec((tk, tk), lambda bi, i: (0, 0)),
        ],
        out_specs=pl.BlockSpec((None, tq, SB_WIDTH), lambda bi, i: (bi, i, 0)),
        out_shape=jax.ShapeDtypeStruct((b, s, SB_WIDTH), BF16),
        scratch_shapes=[pltpu.VMEM((SB_HEADS, SB_HEAD_DIM, tq), F32)],
        compiler_params=_cparams(2), name="sb_attn",
    )(q, k, vt, ntri)


def _gelu_tanh(x):
    return 0.5 * x * (1.0 + jnp.tanh(math.sqrt(2.0 / math.pi) * (x + 0.044715 * (x * x * x))))


def _s5_kernel(u_ref, lr_ref, li_ref, wb_ref, wc_ref, d_ref, wglu_ref, out_ref, p_ref, q_ref, ut_ref, h_ref,
               *, half):
    cs = S5_CHUNK_STATE
    nb = h_ref.shape[0]
    rows = half * nb
    k = pl.program_id(0)
    unit = 8

    @pl.when(k == 0)
    def _():
        p_ref[...] = jnp.zeros_like(p_ref)
        q_ref[...] = jnp.zeros_like(q_ref)
        ut_ref[...] = jnp.zeros_like(ut_ref)
        h_ref[...] = jnp.zeros_like(h_ref)

    def project_pieces(src_ref, hc):
        ys = []

        def chunk(c):
            ys.append(_dot(src_ref[:, 2 * cs * c:2 * cs * (c + 1)].astype(BF16), wc_ref[c]))

        def finish():
            y = _gelu_tanh(jnp.concatenate(ys, axis=1) + d_ref[...] * ut_ref[hc])
            gl = _dot(y.astype(BF16), wglu_ref[...])
            out = gl[:, :S5_WIDTH] * jax.nn.sigmoid(gl[:, S5_WIDTH:])
            out_ref[:, hc * half:(hc + 1) * half, :] = pltpu.einshape("tbc->btc", out.reshape(half, nb, S5_WIDTH))

        return [functools.partial(chunk, c) for c in range(S5_CHUNKS)] + [finish]

    def expand_pieces(hc, dst_ref):
        ub = []

        def load():
            u = pltpu.einshape("btc->tbc", u_ref[:, hc * half:(hc + 1) * half, :]).reshape(rows, S5_WIDTH)
            ut_ref[hc] = u
            ub.append(u.astype(BF16))

        def chunk(c):
            dst_ref[:, 2 * cs * c:2 * cs * (c + 1)] = _dot(ub[0][:, S5_CHUNK_CH * c:S5_CHUNK_CH * (c + 1)],
                                                           wb_ref[c])

        return [load] + [functools.partial(chunk, c) for c in range(S5_CHUNKS)]

    def scan_pieces(buf_ref):
        def piece(c0, t0):
            chunks = (c0, c0 + 1)
            offs = [2 * cs * c for c in chunks]
            lam = [(jnp.broadcast_to(lr_ref[c], (nb, cs)), jnp.broadcast_to(li_ref[c], (nb, cs))) for c in chunks]
            state = [(h_ref[:, off:off + cs], h_ref[:, off + cs:off + 2 * cs]) for off in offs]
            for t in range(t0, t0 + unit):
                r = t * nb
                new = []
                for (lr, li), off, (hr, hi) in zip(lam, offs, state):
                    br = buf_ref[r:r + nb, off:off + cs]
                    bi = buf_ref[r:r + nb, off + cs:off + 2 * cs]
                    nhr = lr * hr - li * hi + br
                    nhi = lr * hi + li * hr + bi
                    buf_ref[r:r + nb, off:off + cs] = nhr
                    buf_ref[r:r + nb, off + cs:off + 2 * cs] = nhi
                    new.append((nhr, nhi))
                state = new
            for off, (hr, hi) in zip(offs, state):
                h_ref[:, off:off + cs] = hr
                h_ref[:, off + cs:off + 2 * cs] = hi

        return [functools.partial(piece, c0, t0) for t0 in range(0, half, unit) for c0 in range(0, S5_CHUNKS, 2)]

    def interleave(mxu, vpu):
        done = 0
        for i, piece in enumerate(mxu):
            piece()
            upto = (i + 1) * len(vpu) // len(mxu)
            for v in vpu[done:upto]:
                v()
            done = upto

    interleave(project_pieces(q_ref, 0) + expand_pieces(0, q_ref), scan_pieces(p_ref))
    h_ref[...] = jnp.where(k == 0, 0.0, h_ref[...])
    interleave(project_pieces(p_ref, 1) + expand_pieces(1, p_ref), scan_pieces(q_ref))


def _s5(u, layer, lr, li, wb, wc, d, wglu, *, steps):
    batch, s, _ = u.shape
    half = steps // 2
    rows = half * batch
    nblocks = s // steps
    const = lambda a: _layer_spec(a, layer)
    width = 2 * S5_CHUNKS * S5_CHUNK_STATE
    return pl.pallas_call(
        functools.partial(_s5_kernel, half=half),
        grid=(nblocks + 1,),
        in_specs=[pl.BlockSpec((batch, steps, S5_WIDTH), lambda i: (0, jnp.minimum(i, nblocks - 1), 0)),
                  const(lr), const(li), const(wb), const(wc), const(d), const(wglu)],
        out_specs=pl.BlockSpec((batch, steps, S5_WIDTH), lambda i: (0, jnp.maximum(i - 1, 0), 0)),
        out_shape=jax.ShapeDtypeStruct((batch, s, S5_WIDTH), F32),
        scratch_shapes=[pltpu.VMEM((rows, width), F32), pltpu.VMEM((rows, width), F32),
                        pltpu.VMEM((2, rows, S5_WIDTH), F32), pltpu.VMEM((batch, width), F32)],
        compiler_params=_cparams(1), name="s5_mixer",
    )(u, lr, li, wb, wc, d, wglu)


def _merge_kernel(x_ref, g_ref, up_ref, o0_ref, l0_ref, o1_ref, l1_ref, o2_ref, l2_ref, ysb_ref, ys5_ref,
                  pw_ref, ps_ref, wg_ref, wb_ref, wo_ref, out_ref, ext_ref, lvl_ref, no1_ref, nl1_ref, no2_ref,
                  nl2_ref, *, tm):
    ti = pl.program_id(1)
    x = x_ref[...]
    xn = _rms(x, g_ref[...]).astype(BF16)

    @pl.when(ti == 0)
    def _():
        ext_ref[:POOL_HALO, :] = jnp.zeros((POOL_HALO, POOL_WIDTH), F32)
        ext_ref[pl.ds(tm + POOL_HALO, POOL_HALO), :] = jnp.zeros((POOL_HALO, POOL_WIDTH), F32)
        lvl_ref[:, pl.ds(tm + POOL_HALO, POOL_HALO), :] = jnp.zeros((2, POOL_HALO, POOL_GROUP), F32)

    ext_ref[pl.ds(POOL_HALO, tm), :] = up_ref[...]
    pos = ti * tm + lax.broadcasted_iota(jnp.int32, (tm, 1), 0)
    ys = []
    for gi, w in enumerate(POOL_WINDOWS):
        c0, c1 = gi * POOL_GROUP, (gi + 1) * POOL_GROUP
        src, have = ext_ref, 1
        while have < w:
            cols = slice(c0, c1) if src is ext_ref else slice(0, POOL_GROUP)
            nxt = src[pl.ds(have, tm + POOL_HALO), cols] + src[pl.ds(0, tm + POOL_HALO), cols]
            dst = lvl_ref.at[have.bit_length() % 2]
            dst[pl.ds(0, tm + POOL_HALO), :] = nxt
            src, have = dst, 2 * have
        off = have - 1
        cols = slice(c0, c1) if src is ext_ref else slice(0, POOL_GROUP)
        acc = src[pl.ds(POOL_HALO - off, tm), cols]
        cnt = jnp.minimum(pos + 1, w).astype(F32)
        p = acc / cnt - ext_ref[pl.ds(POOL_HALO, tm), c0:c1]
        ys.append(_dot(p.astype(BF16), pw_ref[gi]))
    y_pool = jnp.concatenate(ys, axis=1) * ps_ref[...]
    ext_ref[:POOL_HALO, :] = ext_ref[pl.ds(tm, POOL_HALO), :]

    nslab = DIL_GROUP_WIDTH // LANES

    def to_token_order(src_ref, dst_ref):
        dil = src_ref.shape[0]
        for r in range(dil):
            for sl in range(nslab):
                dst_ref[sl, pl.ds(r, tm // dil, stride=dil), :] = src_ref[r, :, sl * LANES:(sl + 1) * LANES]
        return jnp.concatenate([dst_ref[sl] for sl in range(nslab)], axis=1)

    o1, l1 = to_token_order(o1_ref, no1_ref), to_token_order(l1_ref, nl1_ref)
    o2, l2 = to_token_order(o2_ref, no2_ref), to_token_order(l2_ref, nl2_ref)
    l0 = l0_ref[...]
    m = jnp.maximum(jnp.maximum(l0, l1), l2)
    e0, e1, e2 = jnp.exp(l0 - m), jnp.exp(l1 - m), jnp.exp(l2 - m)
    y_dil = (e0 * o0_ref[...] + e1 * o1 + e2 * o2) / (e0 + e1 + e2)

    branches = (y_pool.astype(BF16), y_dil.astype(BF16), ysb_ref[...], ys5_ref[...].astype(BF16))
    merged = jnp.zeros((tm, x.shape[1]), F32)
    for bi, yb in enumerate(branches):
        gate = jax.nn.sigmoid(_dot(xn, wg_ref[bi]))
        merged = merged + gate * _dot(yb, wb_ref[BRANCH_ROWS[bi]:BRANCH_ROWS[bi + 1], :])
    out_ref[...] = x + _dot(merged.astype(BF16), wo_ref[...])


def _merge(x, layer, g, up, o0, l0, o1, l1, o2, l2, ysb, ys5, pw, ps, wg, wb, wo, *, tm):
    b, s, d = x.shape
    dil1, dil2 = o1.shape[1], o2.shape[1]
    gw = DIL_GROUP_WIDTH
    tok = lambda w: pl.BlockSpec((None, tm, w), lambda bi, ti: (bi, ti, 0))
    res = lambda dil: pl.BlockSpec((None, dil, tm // dil, gw), lambda bi, ti: (bi, 0, ti, 0))
    const = lambda a: _layer_spec(a, layer)
    in_specs = [tok(d), const(g), tok(POOL_WIDTH), tok(gw), tok(gw), res(dil1), res(dil1), res(dil2), res(dil2),
                tok(SB_WIDTH), tok(S5_WIDTH),
                const(pw), const(ps), const(wg), const(wb), const(wo)]
    return pl.pallas_call(
        functools.partial(_merge_kernel, tm=tm),
        grid=(b, s // tm), in_specs=in_specs, out_specs=tok(d),
        out_shape=jax.ShapeDtypeStruct((b, s, d), F32),
        scratch_shapes=[pltpu.VMEM((tm + 2 * POOL_HALO, POOL_WIDTH), F32),
                        pltpu.VMEM((2, tm + 2 * POOL_HALO, POOL_GROUP), F32)]
        + [pltpu.VMEM((gw // LANES, tm, LANES), F32)] * 4,
        compiler_params=_cparams(2), name="gated_merge",
    )(x, g, up, o0, l0, o1, l1, o2, l2, ysb, ys5, pw, ps, wg, wb, wo)


def _ffn_kernel(x_ref, g_ref, wup_ref, wdn_ref, fg_ref, out_ref, *, hidden, final):
    x = x_ref[...]
    hn = _rms(x, g_ref[...]).astype(BF16)
    gu = _dot(hn, wup_ref[...])
    h = (jax.nn.silu(gu[:, :hidden]) * gu[:, hidden:]).astype(BF16)
    y = x + _dot(h, wdn_ref[...])
    if final:
        y = _rms(y, fg_ref[...])
    out_ref[...] = y


def _ffn(x2d, layer, g, wup, wdn, fg, *, tm, final):
    n, d = x2d.shape
    hidden = wdn.shape[-2]
    const = lambda a: _layer_spec(a, layer, pipeline_mode=pl.Buffered(1))
    return pl.pallas_call(
        functools.partial(_ffn_kernel, hidden=hidden, final=final),
        grid=(n // tm,),
        in_specs=[pl.BlockSpec((tm, d), lambda i: (i, 0)), const(g), const(wup), const(wdn),
                  pl.BlockSpec(fg.shape, lambda i: (0, 0))],
        out_specs=pl.BlockSpec((tm, d), lambda i: (i, 0)),
        out_shape=jax.ShapeDtypeStruct((n, d), F32),
        compiler_params=_cparams(1), name="swiglu_final" if final else "swiglu",
    )(x2d, g, wup, wdn, fg)


def _t5_bucket(dist):
    exact = REL_BUCKETS // 2
    df = jnp.maximum(dist, 1).astype(F32)
    large = exact + (jnp.log(df / exact) / math.log(REL_MAX_DIST / exact)
                     * (REL_BUCKETS - exact)).astype(jnp.int32)
    large = jnp.minimum(large, REL_BUCKETS - 1)
    return jnp.where(dist < exact, dist, large)


def _dil_bias(rel_bias_g, dil):
    band = DIL_BAND
    i = jnp.arange(band)[:, None]
    c = jnp.arange(2 * band)[None, :]
    dist_sub = band + i - c
    in_band = (dist_sub >= 0) & (dist_sub <= band)
    buckets = _t5_bucket(jnp.clip(dist_sub, 0, band) * dil)
    onehot = (buckets[None] == jnp.arange(REL_BUCKETS)[:, None, None]).astype(F32)
    bias = jnp.einsum('kh,kic->hic', rel_bias_g.astype(F32), onehot, precision=lax.Precision.HIGHEST)
    bias = jnp.where(in_band[None], bias, NEG_INF)
    first = jnp.where((c >= band)[None], bias, NEG_INF)
    return jnp.stack([bias, first])


def _dil_weights(w_dil, group):
    hw = DIL_HEADS * DIL_HEAD_DIM
    cols = [w_dil[..., i * hw + group * DIL_GROUP_WIDTH:i * hw + (group + 1) * DIL_GROUP_WIDTH] for i in range(3)]
    return jnp.concatenate(cols, axis=-1).astype(BF16)


def _s5_params(a_re, a_im, log_dt, b_re, b_im, c_re, c_im):
    lam = lax.complex(a_re.astype(F32), a_im.astype(F32))
    dt = jnp.exp(log_dt.astype(F32))[:, None]
    lam_bar = jnp.exp(lam * dt)
    b_bar = ((lam_bar - 1.0) / lam)[:, :, None] * lax.complex(b_re.astype(F32), b_im.astype(F32))
    gpc = S5_GROUPS // S5_CHUNKS
    eye = jnp.eye(gpc, dtype=F32)

    def pack_b(bm):
        bm = bm.reshape(S5_CHUNKS, gpc, S5_STATE, S5_CH)
        return jnp.einsum('cgpk,gh->cgkhp', bm, eye).reshape(S5_CHUNKS, S5_CHUNK_CH, S5_CHUNK_STATE)

    def pack_c(cm):
        cm = cm.reshape(S5_CHUNKS, gpc, S5_CH, S5_STATE)
        return jnp.einsum('cgkp,gh->cgphk', cm, eye).reshape(S5_CHUNKS, S5_CHUNK_STATE, S5_CHUNK_CH)

    wb = jnp.concatenate([pack_b(jnp.real(b_bar)), pack_b(jnp.imag(b_bar))], axis=2).astype(BF16)
    wc = jnp.concatenate([pack_c(c_re.astype(F32)), pack_c(-c_im.astype(F32))], axis=1).astype(BF16)
    lr = jnp.real(lam_bar).reshape(S5_CHUNKS, 1, S5_CHUNK_STATE)
    li = jnp.imag(lam_bar).reshape(S5_CHUNKS, 1, S5_CHUNK_STATE)
    return lr, li, wb, wc


def kernel(x, attn_norm_g, w_in, pool_w, pool_scale, rel_bias, s5_a_re, s5_a_im, s5_log_dt, s5_b_re, s5_b_im,
           s5_c_re, s5_c_im, s5_d, s5_w_glu, w_branch, w_gate, w_out, ffn_norm_g, w_up, w_down, final_norm_g):
    b, s, d = x.shape
    depth = w_in.shape[0]
    o1 = POOL_WIDTH
    o2 = o1 + 3 * DIL_HEADS * DIL_HEAD_DIM
    o3 = o2 + 3 * SB_WIDTH

    t = SB_TILE
    ntri = -(jnp.arange(t)[None, :] > jnp.arange(t)[:, None]).astype(BF16)
    biases = [_dil_bias(rel_bias[:, g * DIL_HEADS_PER_GROUP:(g + 1) * DIL_HEADS_PER_GROUP], dil)
              for g, (_, dil) in enumerate(DIL_PAIRS)]
    key_head = jnp.arange(DIL_HEADS_PER_GROUP * 2 * DIL_BAND) // (2 * DIL_BAND)
    ones_bd = (key_head[:, None] == (jnp.arange(DIL_GROUP_WIDTH) // DIL_HEAD_DIM)[None, :]).astype(BF16)
    fg = final_norm_g.reshape(1, d).astype(F32)

    w_dil = w_in[:, :, o1:o2]
    w_sb = w_in[:, :, o2:o3]
    in_w = (attn_norm_g.reshape(depth, 1, d).astype(F32), w_in[:, :, :o1].astype(BF16),
            _dil_weights(w_dil, 0), _dil_weights(w_dil, 1), _dil_weights(w_dil, 2),
            w_sb[:, :, :SB_WIDTH].astype(BF16), w_sb[:, :, SB_WIDTH:2 * SB_WIDTH].astype(BF16),
            jnp.swapaxes(w_sb[:, :, 2 * SB_WIDTH:], 1, 2).astype(BF16), w_in[:, :, o3:].astype(BF16))
    s5_w = jax.vmap(_s5_params)(s5_a_re, s5_a_im, s5_log_dt, s5_b_re, s5_b_im, s5_c_re, s5_c_im) + (
        s5_d.reshape(depth, 1, S5_WIDTH).astype(F32), s5_w_glu.astype(BF16))
    merge_w = (pool_w.astype(BF16), pool_scale.reshape(depth, 1, POOL_WIDTH).astype(F32),
               w_gate.astype(BF16), w_branch.astype(BF16), w_out.astype(BF16))
    ffn_w = (ffn_norm_g.reshape(depth, 1, d).astype(F32), w_up.astype(BF16), w_down.astype(BF16))

    for l in range(depth):
        up, d0, d1, d2, q, k, vt, u_s5 = _in_proj(x, l, *in_w, tm=IN_PROJ_ROWS)

        oa0, la0 = _dil_attn(d0.reshape(b, 1, s, d0.shape[-1]), biases[0], ones_bd)
        oa1, la1 = _dil_attn(d1, biases[1], ones_bd)
        oa2, la2 = _dil_attn(d2, biases[2], ones_bd)
        y_sb = _sb_attn(q, k, vt, ntri)
        y_s5 = _s5(u_s5, l, *s5_w, steps=S5_STEPS)

        x = _merge(x, l, in_w[0], up, oa0.reshape(b, s, -1), la0.reshape(b, s, -1), oa1, la1, oa2, la2,
                   y_sb, y_s5, *merge_w, tm=MERGE_ROWS)
        x = _ffn(x.reshape(b * s, d), l, *ffn_w, fg, tm=FFN_ROWS, final=(l == depth - 1)).reshape(b, s, d)
    return x
```
